```python
import jax, jax.numpy as jnp
from jax import lax
import numpy as np

D_MODEL = 1024
BATCH = 16
SEQ = 4096
DEPTH = 4

CHUNK = 64
NORM_EPS = 1e-6
A_WIDTH = D_MODEL // 2
A_EXPAND = 128
A_HEADS = A_WIDTH // A_EXPAND
A_DV = A_WIDTH // A_HEADS
B_HEADS = 4
B_DK = 64
B_DV = 2 * B_DK
B_QK = B_HEADS * B_DK
B_V = B_HEADS * B_DV
ROPE_BASE = 10000.0
C_HEADS = 4
C_DK = 64
C_DV = 128
C_QK = C_HEADS * C_DK
C_V = C_HEADS * C_DV
GK_RANK = 16
GK_NORMALIZER = 16.0
N_BRANCHES = 3
SPLIT_SIZES = (A_WIDTH, A_WIDTH, A_WIDTH, A_WIDTH,
               B_QK, B_QK, B_V, B_V,
               C_QK, C_QK, C_V, C_V, GK_RANK,
               N_BRANCHES * D_MODEL)
PROJ_WIDTH = int(sum(SPLIT_SIZES))
SPLIT_POINTS = tuple(int(s) for s in np.cumsum(SPLIT_SIZES)[:-1])
FFN_HIDDEN = ((8 * D_MODEL + 3 * 256 - 1) // (3 * 256)) * 256

kernel_name = "hybrid_hgrn2_retnet_gla_gated_trunk"


def rmsnorm(x, w, eps=NORM_EPS):
    xf = x.astype(jnp.float32)
    y = xf * lax.rsqrt(jnp.mean(xf * xf, axis=-1, keepdims=True) + eps)
    return (y * w.astype(jnp.float32)).astype(x.dtype)


def head_groupnorm(o, eps=NORM_EPS):
    of = o.astype(jnp.float32)
    mu = jnp.mean(of, axis=-1, keepdims=True)
    var = jnp.mean(jnp.square(of - mu), axis=-1, keepdims=True)
    return ((of - mu) * lax.rsqrt(var + eps)).astype(o.dtype)


def split_heads(t, n_heads):
    b, l, _ = t.shape
    return t.reshape(b, l, n_heads, -1).transpose(0, 2, 1, 3)


def merge_heads(t):
    b, h, l, d = t.shape
    return t.transpose(0, 2, 1, 3).reshape(b, l, h * d)


def rotary_every_two(t):
    seq, dk = t.shape[2], t.shape[3]
    inv_freq = 1.0 / (ROPE_BASE ** jnp.linspace(0.0, 1.0, dk // 2, dtype=jnp.float32))
    ang = jnp.arange(seq, dtype=jnp.float32)[:, None] * inv_freq[None, :]
    sin, cos = jnp.sin(ang), jnp.cos(ang)
    tf = t.astype(jnp.float32)
    t1, t2 = tf[..., 0::2], tf[..., 1::2]
    out = jnp.stack([t1 * cos - t2 * sin, t1 * sin + t2 * cos], axis=-1).reshape(t.shape)
    return out.astype(t.dtype)


def chunked_gated_linear_attention(q, k, v, log_decay):
    bsz, nh, seq, dk = q.shape
    dv = v.shape[-1]
    dg = log_decay.shape[-1]
    n_chunks = seq // CHUNK

    def to_chunks(t):
        return t.reshape(bsz, nh, n_chunks, CHUNK, t.shape[-1]).transpose(2, 0, 1, 3, 4)

    qc, kc, vc = to_chunks(q), to_chunks(k), to_chunks(v)
    gc = to_chunks(log_decay.astype(jnp.float32))
    causal = jnp.tril(jnp.ones((CHUNK, CHUNK), dtype=bool))[:, :, None]

    def step(state, chunk):
        qi, ki, vi, gi = chunk
        b = jnp.cumsum(gi, axis=2)
        b_last = b[:, :, -1:, :]
        o_inter = jnp.einsum('bhcd,bhdv->bhcv', qi * jnp.exp(b), state)
        rel = jnp.where(causal, b[:, :, :, None, :] - b[:, :, None, :, :], -jnp.inf)
        if dg == 1:
            scores = jnp.einsum('bhid,bhjd->bhij', qi, ki) * jnp.exp(rel[..., 0])
        else:
            scores = jnp.sum(qi[:, :, :, None, :] * ki[:, :, None, :, :] * jnp.exp(rel), axis=-1)
        o = o_inter + jnp.einsum('bhij,bhjv->bhiv', scores, vi)
        state = state * jnp.swapaxes(jnp.exp(b_last), -1, -2) + jnp.einsum(
            'bhcd,bhcv->bhdv', ki * jnp.exp(b_last - b), vi)
        return state, o

    state0 = jnp.zeros((bsz, nh, dk, dv), jnp.float32)
    _, oc = lax.scan(step, state0, (qc, kc, vc, gc))
    return oc.transpose(1, 2, 0, 3, 4).reshape(bsz, nh, seq, dv).astype(v.dtype)


def hybrid_mixer(xn, w_in, lb, w_gk_up, b_gk, gn_a, gn_c, w_br_a, w_br_b, w_br_c, w_out):
    bsz, seq, _ = xn.shape
    proj = jnp.einsum('bld,dp->blp', xn, w_in)
    (a_q, a_f, a_i, a_g, b_q, b_k, b_v, b_g,
     c_q, c_k, c_v, c_g, c_r, gate_logits) = jnp.split(proj, SPLIT_POINTS, axis=-1)

    zf = a_f.astype(jnp.float32)
    lbf = lb.astype(jnp.float32)
    log_f = jnp.logaddexp(jnp.log(lbf), jnp.log1p(-lbf) + jax.nn.log_sigmoid(zf))
    k_a = ((1.0 - lbf) * jax.nn.sigmoid(-zf)).astype(xn.dtype)
    q_a = split_heads(jax.nn.silu(a_q), A_HEADS) * (A_EXPAND ** -0.5)
    o_a = chunked_gated_linear_attention(q_a, split_heads(k_a, A_HEADS), split_heads(a_i, A_HEADS),
                                         split_heads(log_f, A_HEADS))
    o_a = merge_heads(rmsnorm(o_a, gn_a)) * jax.nn.silu(a_g)

    q_b = rotary_every_two(split_heads(b_q, B_HEADS))
    k_b = rotary_every_two(split_heads(b_k, B_HEADS)) * (B_DK ** -0.5)
    log_gamma = jnp.log(1.0 - 2.0 ** (-5.0 - jnp.arange(B_HEADS, dtype=jnp.float32)))
    g_b = jnp.broadcast_to(log_gamma[None, :, None, None], (bsz, B_HEADS, seq, 1))
    o_b = chunked_gated_linear_attention(q_b, k_b, split_heads(b_v, B_HEADS), g_b)
    o_b = merge_heads(head_groupnorm(o_b)) * jax.nn.silu(b_g)

    gk_logits = jnp.einsum('blr,rk->blk', c_r, w_gk_up) + b_gk
    log_gk = jax.nn.log_sigmoid(gk_logits.astype(jnp.float32)) / GK_NORMALIZER
    q_c = split_heads(c_q, C_HEADS) * (C_DK ** -0.5)
    o_c = chunked_gated_linear_attention(q_c, split_heads(c_k, C_HEADS), split_heads(c_v, C_HEADS),
                                         split_heads(log_gk, C_HEADS))
    o_c = merge_heads(rmsnorm(o_c, gn_c)) * jax.nn.silu(c_g)

    gates = jax.nn.sigmoid(gate_logits).reshape(bsz, seq, N_BRANCHES, D_MODEL)
    merged = (gates[:, :, 0] * jnp.einsum('blc,cd->bld', o_a, w_br_a)
              + gates[:, :, 1] * jnp.einsum('blc,cd->bld', o_b, w_br_b)
              + gates[:, :, 2] * jnp.einsum('blc,cd->bld', o_c, w_br_c))
    return jnp.einsum('bld,de->ble', merged, w_out)


def swiglu_ffn(xn, w_gate, w_up, w_down):
    h = jax.nn.silu(jnp.einsum('bld,df->blf', xn, w_gate)) * jnp.einsum('bld,df->blf', xn, w_up)
    return jnp.einsum('blf,fd->bld', h, w_down)


def setup_inputs(seed: int = 0) -> dict:
    key = jax.random.key(seed)
    ks = jax.random.split(key, 17)
    f32 = jnp.float32
    nrm = jax.random.normal
    return {
        "x": nrm(ks[0], (BATCH, SEQ, D_MODEL), f32),
        "norm_mix": 1.0 + 0.02 * nrm(ks[1], (DEPTH, D_MODEL), f32),
        "w_in": nrm(ks[2], (DEPTH, D_MODEL, PROJ_WIDTH), f32) * D_MODEL ** -0.5,
        "lb_logits": 0.1 * nrm(ks[3], (DEPTH, A_WIDTH), f32),
        "w_gk_up": nrm(ks[4], (DEPTH, GK_RANK, C_QK), f32) * GK_RANK ** -0.5,
        "b_gk": 0.1 * nrm(ks[5], (DEPTH, C_QK), f32),
        "gn_a": 1.0 + 0.02 * nrm(ks[6], (DEPTH, A_DV), f32),
        "gn_c": 1.0 + 0.02 * nrm(ks[7], (DEPTH, C_DV), f32),
        "w_br_a": nrm(ks[8], (DEPTH, A_WIDTH, D_MODEL), f32) * A_WIDTH ** -0.5,
        "w_br_b": nrm(ks[9], (DEPTH, B_V, D_MODEL), f32) * B_V ** -0.5,
        "w_br_c": nrm(ks[10], (DEPTH, C_V, D_MODEL), f32) * C_V ** -0.5,
        "w_out": nrm(ks[11], (DEPTH, D_MODEL, D_MODEL), f32) * D_MODEL ** -0.5,
        "norm_ffn": 1.0 + 0.02 * nrm(ks[12], (DEPTH, D_MODEL), f32),
        "w_ffn_gate": nrm(ks[13], (DEPTH, D_MODEL, FFN_HIDDEN), f32) * D_MODEL ** -0.5,
        "w_ffn_up": nrm(ks[14], (DEPTH, D_MODEL, FFN_HIDDEN), f32) * D_MODEL ** -0.5,
        "w_ffn_down": nrm(ks[15], (DEPTH, FFN_HIDDEN, D_MODEL), f32) * FFN_HIDDEN ** -0.5,
        "norm_final": 1.0 + 0.02 * nrm(ks[16], (D_MODEL,), f32),
    }


def reference(x, norm_mix, w_in, lb_logits, w_gk_up, b_gk, gn_a, gn_c, w_br_a, w_br_b, w_br_c,
              w_out, norm_ffn, w_ffn_gate, w_ffn_up, w_ffn_down, norm_final):
    lb_all = jnp.cumsum(jax.nn.softmax(lb_logits.astype(jnp.float32), axis=0), axis=0)
    lb_all = lb_all - lb_all[0:1]
    for layer in range(DEPTH):
        h = x + hybrid_mixer(rmsnorm(x, norm_mix[layer]), w_in[layer], lb_all[layer], w_gk_up[layer],
                             b_gk[layer], gn_a[layer], gn_c[layer], w_br_a[layer], w_br_b[layer],
                             w_br_c[layer], w_out[layer])
        x = h + swiglu_ffn(rmsnorm(h, norm_ffn[layer]), w_ffn_gate[layer], w_ffn_up[layer],
                           w_ffn_down[layer])
    return rmsnorm(x, norm_final)
```

```python
import functools

import numpy as np
import jax
import jax.numpy as jnp
from jax import lax
from jax.experimental import pallas as pl
from jax.experimental.pallas import tpu as pltpu

F32 = jnp.float32
BF16 = jnp.bfloat16

DEPTH = 4
CHUNK = 64
SUB = 8
NORM_EPS = 1e-6
N_HEADS = 4
HEAD_DV = 128
ROPE_BASE = 10000.0
GK_RANK = 16
GK_NORMALIZER = 16.0
LANES = 128
V7X_SCOPED_VMEM_BYTES = 60000 * 1024

_A_Q, _A_F, _A_I, _A_G = 0, 512, 1024, 1536
_B_Q, _B_K, _B_V, _B_G = 2048, 2304, 2560, 3072
_C_Q, _C_K, _C_V, _C_G = 3584, 3840, 4096, 4608
_C_R = 5120
_GATES = 5248
_PACKED_WIDTH = 8320


def _nt(a, b):
    return lax.dot_general(a, b, (((1,), (1,)), ((), ())), preferred_element_type=F32)


def _tn(a, b):
    return lax.dot_general(a, b, (((0,), (0,)), ((), ())), preferred_element_type=F32)


def _mm(a, b):
    return jnp.dot(a, b, preferred_element_type=F32)


def _sigmoid(z):
    return 1.0 / (1.0 + jnp.exp(-z))


def _log_sigmoid(z):
    return jnp.minimum(z, 0.0) - jnp.log1p(jnp.exp(-jnp.abs(z)))


def _rmsnorm(x, w):
    ms = jnp.mean(x * x, axis=-1, keepdims=True)
    return x * lax.rsqrt(ms + NORM_EPS) * w


def _swap_rot_halves(t):
    n = t.shape[1]
    lane = lax.broadcasted_iota(jnp.int32, t.shape, 1)
    first = (lane % 64) < 32
    return jnp.where(first, pltpu.roll(t, n - 32, 1), pltpu.roll(t, 32, 1))


def _inproj_kernel(layer, x_ref, nw_ref, w_ref, lbl_ref, cos_ref, sin_ref, wup_ref, bgk_ref,
                   qa_ref, ka_ref, va_ref, lfa_ref, oga_ref,
                   qb_ref, kb_ref, vb_ref, ogb_ref,
                   qc_ref, kc_ref, vc_ref, lgc_ref, ogc_ref, gt_ref):
    xn = _rmsnorm(x_ref[...], nw_ref[...]).astype(BF16)

    def proj(lo, hi):
        return _mm(xn, w_ref[:, lo:hi])

    lbl = lbl_ref[...]
    e = jnp.exp(lbl - jnp.max(lbl, axis=0, keepdims=True))
    p = e / jnp.sum(e, axis=0, keepdims=True)
    c = p[0:1]
    for r in range(1, layer + 1):
        c = c + p[r:r + 1]
    lb = c - p[0:1]

    aq = proj(_A_Q, _A_F)
    qa_ref[...] = (aq * _sigmoid(aq) * (HEAD_DV ** -0.5)).astype(BF16)
    z = proj(_A_F, _A_I)
    ls = jnp.log1p(-lb) + _log_sigmoid(z)
    la = jnp.log(jnp.where(lb > 0.0, lb, 1.0))
    both = jnp.maximum(la, ls) + jnp.log1p(jnp.exp(-jnp.abs(la - ls)))
    lfa_ref[...] = jnp.where(lb > 0.0, both, ls)
    ka_ref[...] = ((1.0 - lb) * _sigmoid(-z)).astype(BF16)
    va_ref[...] = proj(_A_I, _A_G).astype(BF16)
    ga = proj(_A_G, _B_Q)
    oga_ref[...] = (ga * _sigmoid(ga)).astype(BF16)

    cos = cos_ref[...]
    sin = sin_ref[...]
    qb = proj(_B_Q, _B_K)
    qb_ref[...] = (qb * cos + _swap_rot_halves(qb) * sin).astype(BF16)
    kb = proj(_B_K, _B_V)
    kb_ref[...] = ((kb * cos + _swap_rot_halves(kb) * sin) * (64 ** -0.5)).astype(BF16)
    vb_ref[...] = proj(_B_V, _B_G).astype(BF16)
    gb = proj(_B_G, _C_Q)
    ogb_ref[...] = (gb * _sigmoid(gb)).astype(BF16)

    qc_ref[...] = (proj(_C_Q, _C_K) * (64 ** -0.5)).astype(BF16)
    kc_ref[...] = proj(_C_K, _C_V).astype(BF16)
    vc_ref[...] = proj(_C_V, _C_G).astype(BF16)
    gc = proj(_C_G, _C_R)
    ogc_ref[...] = (gc * _sigmoid(gc)).astype(BF16)
    code = proj(_C_R, _GATES).astype(BF16)
    gk = _mm(code, wup_ref[...]) + bgk_ref[...]
    lgc_ref[...] = _log_sigmoid(gk) * (1.0 / GK_NORMALIZER)

    gt_ref[...] = proj(_GATES, _PACKED_WIDTH).astype(BF16)


def _inproj(layer, xf, nw, w, lbl, cos_t, sin_t, wup, bgk, seq_len, tm):
    tokens, d = xf.shape
    nt = tokens // tm
    per_seq = seq_len // tm

    def rows(width, dtype):
        return jax.ShapeDtypeStruct((tokens, width), dtype), pl.BlockSpec((tm, width), lambda i: (i, 0))

    outs = [rows(512, BF16), rows(512, BF16), rows(512, BF16), rows(512, F32), rows(512, BF16),
            rows(256, BF16), rows(256, BF16), rows(512, BF16), rows(512, BF16),
            rows(256, BF16), rows(256, BF16), rows(512, BF16), rows(256, F32), rows(512, BF16),
            rows(3072, BF16)]
    const = lambda shape: pl.BlockSpec(shape, lambda i: (0,) * len(shape), pipeline_mode=pl.Buffered(1))
    in_specs = [
        pl.BlockSpec((tm, d), lambda i: (i, 0)),
        const((1, d)),
        const(w.shape),
        const(lbl.shape),
        pl.BlockSpec((tm, 256), lambda i: (i % per_seq, 0)),
        pl.BlockSpec((tm, 256), lambda i: (i % per_seq, 0)),
        const(wup.shape),
        const(bgk.shape),
    ]
    return pl.pallas_call(
        functools.partial(_inproj_kernel, layer),
        grid=(nt,),
        in_specs=in_specs,
        out_specs=[o[1] for o in outs],
        out_shape=[o[0] for o in outs],
        compiler_params=pltpu.CompilerParams(
            dimension_semantics=("parallel",), vmem_limit_bytes=V7X_SCOPED_VMEM_BYTES),
        name="inproj",
    )(xf, nw, w, lbl, cos_t, sin_t, wup, bgk)


def _head_masks(nh):
    lane = lax.broadcasted_iota(jnp.int32, (1, LANES), 1)
    return [(lane // (LANES // nh)) == m for m in range(nh)]


def _stack_heads(x, masks):
    if len(masks) == 1:
        return x
    return jnp.concatenate([jnp.where(m, x, 0.0) for m in masks], axis=0)


def _cumsum_rows_f32(tri, g):
    hi = g.astype(BF16)
    r1 = g - hi.astype(F32)
    mid = r1.astype(BF16)
    lo = (r1 - mid.astype(F32)).astype(BF16)
    return _mm(tri, hi) + _mm(tri, mid) + _mm(tri, lo)


def _gla_group_scores(q, k, b, masks):
    nh = len(masks)
    lane_c = lax.broadcasted_iota(jnp.int32, (SUB, CHUNK), 1)
    row_c = lax.broadcasted_iota(jnp.int32, (SUB, CHUNK), 0)
    blocks = [[] for _ in range(nh)]
    for blk in range(CHUNK // SUB):
        lo = blk * SUB
        qi = q[lo:lo + SUB]
        bi = b[lo:lo + SUB]
        dg = [jnp.zeros((SUB, CHUNK), F32) for _ in range(nh)]
        for jj in range(SUB):
            j = lo + jj
            pr = qi * k[j:j + 1] * jnp.exp(jnp.minimum(bi - b[j:j + 1], 0.0))
            for m in range(nh):
                pm = pr if nh == 1 else jnp.where(masks[m], pr, 0.0)
                col = jnp.sum(pm, axis=-1, keepdims=True)
                dg[m] = jnp.where(lane_c == j, col, dg[m])
        causal = (lane_c - lo) <= row_c
        dg = [jnp.where(causal, d, 0.0) for d in dg]
        if blk > 0:
            ref_row = b[lo - 1:lo]
            qp = qi * jnp.exp(bi - ref_row)
            kk = k[0:lo] * jnp.exp(ref_row - b[0:lo])
            kk = jnp.concatenate([kk, jnp.zeros((CHUNK - lo, LANES), F32)], axis=0).astype(BF16)
            lhs = _stack_heads(qp, masks)
            if lhs.shape[0] < 16:
                lhs = jnp.concatenate([lhs, jnp.zeros((16 - lhs.shape[0], LANES), F32)], axis=0)
            off = _nt(lhs.astype(BF16), kk)
            dg = [off[m * SUB:(m + 1) * SUB] + dg[m] for m in range(nh)]
        for m in range(nh):
            blocks[m].append(dg[m])
    return [jnp.concatenate(bl, axis=0) for bl in blocks]


def _mix_gla_kernel(nh, q_ref, k_ref, v_ref, g_ref, og_ref, gn_ref, o_ref, st_ref):
    @pl.when(pl.program_id(1) == 0)
    def _():
        st_ref[...] = jnp.zeros_like(st_ref)

    n_groups = q_ref.shape[1] // LANES
    masks = _head_masks(nh)
    tri = (lax.broadcasted_iota(jnp.int32, (CHUNK, CHUNK), 0)
           >= lax.broadcasted_iota(jnp.int32, (CHUNK, CHUNK), 1)).astype(BF16)
    gn = gn_ref[...]

    def chunk_body(c, carry):
        rows = pl.ds(pl.multiple_of(c * CHUNK, CHUNK), CHUNK)
        b_all = _cumsum_rows_f32(tri, g_ref[rows, :])
        for gi in range(n_groups):
            lanes = slice(gi * LANES, (gi + 1) * LANES)
            q = q_ref[rows, lanes].astype(F32)
            k = k_ref[rows, lanes].astype(F32)
            b = b_all[:, lanes]
            b_last = b[CHUNK - 1:CHUNK]
            scores = _gla_group_scores(q, k, b, masks)
            st = st_ref[gi]
            inter = _nt(_stack_heads(q * jnp.exp(b), masks).astype(BF16), st.astype(BF16))
            vs = []
            for m in range(nh):
                vl = slice((gi * nh + m) * HEAD_DV, (gi * nh + m + 1) * HEAD_DV)
                v = v_ref[rows, vl]
                vs.append(v)
                o = _mm(scores[m].astype(BF16), v) + inter[m * CHUNK:(m + 1) * CHUNK]
                y = _rmsnorm(o, gn) * og_ref[rows, vl].astype(F32)
                o_ref[rows, vl] = y.astype(BF16)
            kd = _stack_heads(k * jnp.exp(b_last - b), masks).astype(BF16)
            vcat = vs[0] if nh == 1 else jnp.concatenate(vs, axis=0)
            st_ref[gi] = st * jnp.exp(b_last) + _tn(vcat, kd)
        return carry

    lax.fori_loop(0, q_ref.shape[0] // CHUNK, chunk_body, 0)


def _mix_ret_kernel(q_ref, k_ref, v_ref, og_ref, dmask_ref, qdec_ref, kdec_ref, sdec_ref, o_ref, st_ref):
    @pl.when(pl.program_id(1) == 0)
    def _():
        st_ref[...] = jnp.zeros_like(st_ref)

    nh = 2
    n_groups = q_ref.shape[1] // LANES
    masks = _head_masks(nh)

    def chunk_body(c, carry):
        rows = pl.ds(pl.multiple_of(c * CHUNK, CHUNK), CHUNK)
        for gi in range(n_groups):
            lanes = slice(gi * LANES, (gi + 1) * LANES)
            q = q_ref[rows, lanes].astype(F32)
            k = k_ref[rows, lanes].astype(F32)
            sc = _nt(_stack_heads(q, masks).astype(BF16), k.astype(BF16)) * dmask_ref[gi]
            st = st_ref[gi]
            inter = _nt(_stack_heads(q * qdec_ref[gi], masks).astype(BF16), st.astype(BF16))
            vs = []
            for m in range(nh):
                vl = slice((gi * nh + m) * HEAD_DV, (gi * nh + m + 1) * HEAD_DV)
                v = v_ref[rows, vl]
                vs.append(v)
                o = _mm(sc[m * CHUNK:(m + 1) * CHUNK].astype(BF16), v) + inter[m * CHUNK:(m + 1) * CHUNK]
                mu = jnp.mean(o, axis=-1, keepdims=True)
                var = jnp.mean(jnp.square(o - mu), axis=-1, keepdims=True)
                y = (o - mu) * lax.rsqrt(var + NORM_EPS) * og_ref[rows, vl].astype(F32)
                o_ref[rows, vl] = y.astype(BF16)
            kd = _stack_heads(k * kdec_ref[gi], masks).astype(BF16)
            st_ref[gi] = st * sdec_ref[gi] + _tn(jnp.concatenate(vs, axis=0), kd)
        return carry

    lax.fori_loop(0, q_ref.shape[0] // CHUNK, chunk_body, 0)


def _mixer_call(body, name, row_inputs, const_inputs, batch, seq_len, tb, n_groups):
    tokens = batch * seq_len
    per_seq = seq_len // tb
    row_spec = lambda a: pl.BlockSpec((tb, a.shape[1]), lambda bi, ti: (bi * per_seq + ti, 0))
    const_spec = lambda a: pl.BlockSpec(a.shape, lambda bi, ti: (0,) * a.ndim)
    width = N_HEADS * HEAD_DV
    return pl.pallas_call(
        body,
        grid=(batch, per_seq),
        in_specs=[row_spec(a) for a in row_inputs] + [const_spec(a) for a in const_inputs],
        out_specs=pl.BlockSpec((tb, width), lambda bi, ti: (bi * per_seq + ti, 0)),
        out_shape=jax.ShapeDtypeStruct((tokens, width), BF16),
        scratch_shapes=[pltpu.VMEM((n_groups, HEAD_DV, LANES), F32)],
        compiler_params=pltpu.CompilerParams(
            dimension_semantics=("parallel", "arbitrary"), vmem_limit_bytes=V7X_SCOPED_VMEM_BYTES),
        name=name,
    )(*row_inputs, *const_inputs)


def _post_kernel(final, oa_ref, ob_ref, oc_ref, gt_ref, x_ref, wbr_ref, wout_ref, nffn_ref,
                 wg_ref, wu_ref, wd_ref, nfin_ref, out_ref):
    d = x_ref.shape[1]
    merged = None
    for i, o_ref in enumerate((oa_ref, ob_ref, oc_ref)):
        gate = _sigmoid(gt_ref[:, i * d:(i + 1) * d].astype(F32))
        term = gate * _mm(o_ref[...], wbr_ref[i])
        merged = term if merged is None else merged + term
    h = x_ref[...] + _mm(merged.astype(BF16), wout_ref[...])
    hn = _rmsnorm(h, nffn_ref[...]).astype(BF16)
    g = _mm(hn, wg_ref[...])
    u = _mm(hn, wu_ref[...])
    act = (g * _sigmoid(g) * u).astype(BF16)
    y = h + _mm(act, wd_ref[...])
    if final:
        y = _rmsnorm(y, nfin_ref[...])
    out_ref[...] = y


def _post(final, oa, ob, oc, gt, xf, wbr, wout, nffn, wg, wu, wd, nfin, tm):
    tokens, d = xf.shape
    row = lambda a: pl.BlockSpec((tm, a.shape[1]), lambda i: (i, 0))
    const = lambda a: pl.BlockSpec(a.shape, lambda i: (0,) * a.ndim, pipeline_mode=pl.Buffered(1))
    return pl.pallas_call(
        functools.partial(_post_kernel, final),
        grid=(tokens // tm,),
        in_specs=[row(oa), row(ob), row(oc), row(gt), row(xf),
                  const(wbr), const(wout), const(nffn), const(wg), const(wu), const(wd), const(nfin)],
        out_specs=pl.BlockSpec((tm, d), lambda i: (i, 0)),
        out_shape=jax.ShapeDtypeStruct((tokens, d), F32),
        compiler_params=pltpu.CompilerParams(
            dimension_semantics=("parallel",), vmem_limit_bytes=V7X_SCOPED_VMEM_BYTES),
        name="post",
    )(oa, ob, oc, gt, xf, wbr, wout, nffn, wg, wu, wd, nfin)


def _rotary_tables(seq_len):
    inv_freq = 1.0 / (ROPE_BASE ** jnp.linspace(0.0, 1.0, 32, dtype=F32))
    ang = jnp.arange(seq_len, dtype=F32)[:, None] * inv_freq[None, :]
    sin, cos = jnp.sin(ang), jnp.cos(ang)
    cos_t = jnp.tile(jnp.concatenate([cos, cos], axis=1), (1, N_HEADS))
    sin_t = jnp.tile(jnp.concatenate([-sin, sin], axis=1), (1, N_HEADS))
    return cos_t, sin_t


def _retention_tables():
    log_gamma = jnp.log(1.0 - 2.0 ** (-5.0 - jnp.arange(N_HEADS, dtype=F32)))
    pos = jnp.arange(CHUNK, dtype=F32)
    diff = pos[:, None] - pos[None, :]
    dmask = jnp.where(diff >= 0, jnp.exp(diff[None] * log_gamma[:, None, None]), 0.0)
    lane_gamma = jnp.repeat(log_gamma, 64).reshape(N_HEADS // 2, 1, LANES)
    qdec = jnp.exp((pos[None, :, None] + 1.0) * lane_gamma)
    kdec = jnp.exp((CHUNK - 1.0 - pos[None, :, None]) * lane_gamma)
    sdec = jnp.exp(float(CHUNK) * lane_gamma)
    return dmask.reshape(N_HEADS // 2, 2 * CHUNK, CHUNK), qdec, kdec, sdec


def _pack_w_in(w_in):
    sizes = (512, 512, 512, 512, 256, 256, 512, 512, 256, 256, 512, 512, GK_RANK, 3072)
    offs = np.concatenate([[0], np.cumsum(sizes)])
    seg = [w_in[:, :, offs[i]:offs[i + 1]] for i in range(len(sizes))]
    perm = np.concatenate([np.concatenate([64 * h + np.arange(0, 64, 2), 64 * h + np.arange(1, 64, 2)])
                           for h in range(N_HEADS)])
    seg[4] = seg[4][:, :, perm]
    seg[5] = seg[5][:, :, perm]
    seg[12] = jnp.pad(seg[12], ((0, 0), (0, 0), (0, LANES - GK_RANK)))
    return jnp.concatenate(seg, axis=2).astype(BF16)


def kernel(x, norm_mix, w_in, lb_logits, w_gk_up, b_gk, gn_a, gn_c, w_br_a, w_br_b, w_br_c, w_out,
           norm_ffn, w_ffn_gate, w_ffn_up, w_ffn_down, norm_final):
    batch, seq_len, d = x.shape
    tokens = batch * seq_len
    tm = min(256, seq_len)
    tb = min(256, seq_len)

    w_packed = _pack_w_in(w_in)
    assert w_packed.shape[2] == _PACKED_WIDTH
    wup = jnp.pad(w_gk_up, ((0, 0), (0, LANES - GK_RANK), (0, 0))).astype(BF16)
    wbr = jnp.stack([w_br_a, w_br_b, w_br_c], axis=1).astype(BF16)
    wout = w_out.astype(BF16)
    wg, wu, wd = w_ffn_gate.astype(BF16), w_ffn_up.astype(BF16), w_ffn_down.astype(BF16)
    cos_t, sin_t = _rotary_tables(seq_len)
    dmask, qdec, kdec, sdec = _retention_tables()
    nfin = norm_final.reshape(1, d)

    xf = x.reshape(tokens, d)
    for layer in range(DEPTH):
        (qa, ka, va, lfa, oga, qb, kb, vb, ogb, qc, kc, vc, lgc, ogc, gt) = _inproj(
            layer, xf, norm_mix[layer].reshape(1, d), w_packed[layer], lb_logits, cos_t, sin_t,
            wup[layer], b_gk[layer].reshape(1, -1), seq_len, tm)
        oa = _mixer_call(functools.partial(_mix_gla_kernel, 1), "mix_a",
                         [qa, ka, va, lfa, oga], [gn_a[layer].reshape(1, -1)], batch, seq_len, tb, 4)
        ob = _mixer_call(_mix_ret_kernel, "mix_b",
                         [qb, kb, vb, ogb], [dmask, qdec, kdec, sdec], batch, seq_len, tb, 2)
        oc = _mixer_call(functools.partial(_mix_gla_kernel, 2), "mix_c",
                         [qc, kc, vc, lgc, ogc], [gn_c[layer].reshape(1, -1)], batch, seq_len, tb, 2)
        xf = _post(layer == DEPTH - 1, oa, ob, oc, gt, xf, wbr[layer], wout[layer],
                   norm_ffn[layer].reshape(1, d), wg[layer], wu[layer], wd[layer], nfin, tm)
    return xf.reshape(batch, seq_len, d)
```

```python
import functools

import numpy as np
import jax
import jax.numpy as jnp
from jax import lax
from jax.experimental import pallas as pl
from jax.experimental.pallas import tpu as pltpu

F32 = jnp.float32
BF16 = jnp.bfloat16

DEPTH = 4
CHUNK = 64
SUB = 8
NORM_EPS = 1e-6
N_HEADS = 4
HEAD_DV = 128
ROPE_BASE = 10000.0
GK_RANK = 16
GK_NORMALIZER = 16.0
LOG2E = 1.4426950408889634
LANES = 128
V7X_SCOPED_VMEM_BYTES = 60000 * 1024

_A_Q, _A_F, _A_I, _A_G = 0, 512, 1024, 1536
_B_Q, _B_K, _B_V, _B_G = 2048, 2304, 2560, 3072
_C_Q, _C_K, _C_V, _C_G = 3584, 3840, 4096, 4608
_C_R = 5120
_GATES = 5248
_PACKED_WIDTH = 8320


def _nt(a, b):
    return lax.dot_general(a, b, (((1,), (1,)), ((), ())), preferred_element_type=F32)


def _tn(a, b):
    return lax.dot_general(a, b, (((0,), (0,)), ((), ())), preferred_element_type=F32)


def _mm(a, b):
    return jnp.dot(a, b, preferred_element_type=F32)


def _sigmoid(z):
    return 1.0 / (1.0 + jnp.exp(-z))


def _log_sigmoid(z):
    return jnp.minimum(z, 0.0) - jnp.log1p(jnp.exp(-jnp.abs(z)))


def _rmsnorm(x, w):
    ms = jnp.mean(x * x, axis=-1, keepdims=True)
    return x * lax.rsqrt(ms + NORM_EPS) * w


def _swap_rot_halves(t):
    n = t.shape[1]
    lane = lax.broadcasted_iota(jnp.int32, t.shape, 1)
    first = (lane % 64) < 32
    return jnp.where(first, pltpu.roll(t, n - 32, 1), pltpu.roll(t, 32, 1))


def _inproj_kernel(layer, x_ref, nw_ref, w_ref, lbl_ref, cos_ref, sin_ref, wup_ref, bgk_ref,
                   qa_ref, ka_ref, va_ref, lfa_ref, oga_ref,
                   qb_ref, kb_ref, vb_ref, ogb_ref,
                   qc_ref, kc_ref, vc_ref, lgc_ref, ogc_ref, gt_ref):
    xn = _rmsnorm(x_ref[...], nw_ref[...]).astype(BF16)

    def proj(lo, hi):
        return _mm(xn, w_ref[:, lo:hi])

    lbl = lbl_ref[...]
    e = jnp.exp(lbl - jnp.max(lbl, axis=0, keepdims=True))
    p = e / jnp.sum(e, axis=0, keepdims=True)
    c = p[0:1]
    for r in range(1, layer + 1):
        c = c + p[r:r + 1]
    lb = c - p[0:1]

    aq = proj(_A_Q, _A_F)
    qa_ref[...] = (aq * _sigmoid(aq) * (HEAD_DV ** -0.5)).astype(BF16)
    z = proj(_A_F, _A_I)
    ls = jnp.log1p(-lb) + _log_sigmoid(z)
    la = jnp.log(jnp.where(lb > 0.0, lb, 1.0))
    both = jnp.maximum(la, ls) + jnp.log1p(jnp.exp(-jnp.abs(la - ls)))
    lfa_ref[...] = jnp.where(lb > 0.0, both, ls)
    ka_ref[...] = ((1.0 - lb) * _sigmoid(-z)).astype(BF16)
    va_ref[...] = proj(_A_I, _A_G).astype(BF16)
    ga = proj(_A_G, _B_Q)
    oga_ref[...] = (ga * _sigmoid(ga)).astype(BF16)

    cos = cos_ref[...]
    sin = sin_ref[...]
    qb = proj(_B_Q, _B_K)
    qb_ref[...] = (qb * cos + _swap_rot_halves(qb) * sin).astype(BF16)
    kb = proj(_B_K, _B_V)
    kb_ref[...] = ((kb * cos + _swap_rot_halves(kb) * sin) * (64 ** -0.5)).astype(BF16)
    vb_ref[...] = proj(_B_V, _B_G).astype(BF16)
    gb = proj(_B_G, _C_Q)
    ogb_ref[...] = (gb * _sigmoid(gb)).astype(BF16)

    qc_ref[...] = (proj(_C_Q, _C_K) * (64 ** -0.5)).astype(BF16)
    kc_ref[...] = proj(_C_K, _C_V).astype(BF16)
    vc_ref[...] = proj(_C_V, _C_G).astype(BF16)
    gc = proj(_C_G, _C_R)
    ogc_ref[...] = (gc * _sigmoid(gc)).astype(BF16)
    code = proj(_C_R, _GATES).astype(BF16)
    gk = _mm(code, wup_ref[...]) + bgk_ref[...]
    lgc_ref[...] = _log_sigmoid(gk) * (1.0 / GK_NORMALIZER)

    gt_ref[...] = proj(_GATES, _PACKED_WIDTH).astype(BF16)


def _inproj(layer, xf, nw, w, lbl, cos_t, sin_t, wup, bgk, seq_len, tm):
    tokens, d = xf.shape
    nt = tokens // tm
    per_seq = seq_len // tm

    def rows(width, dtype):
        return jax.ShapeDtypeStruct((tokens, width), dtype), pl.BlockSpec((tm, width), lambda i: (i, 0))

    outs = [rows(512, BF16), rows(512, BF16), rows(512, BF16), rows(512, F32), rows(512, BF16),
            rows(256, BF16), rows(256, BF16), rows(512, BF16), rows(512, BF16),
            rows(256, BF16), rows(256, BF16), rows(512, BF16), rows(256, F32), rows(512, BF16),
            rows(3072, BF16)]
    const = lambda shape: pl.BlockSpec(shape, lambda i: (0,) * len(shape), pipeline_mode=pl.Buffered(1))
    in_specs = [
        pl.BlockSpec((tm, d), lambda i: (i, 0)),
        const((1, d)),
        const(w.shape),
        const(lbl.shape),
        pl.BlockSpec((tm, 256), lambda i: (i % per_seq, 0)),
        pl.BlockSpec((tm, 256), lambda i: (i % per_seq, 0)),
        const(wup.shape),
        const(bgk.shape),
    ]
    return pl.pallas_call(
        functools.partial(_inproj_kernel, layer),
        grid=(nt,),
        in_specs=in_specs,
        out_specs=[o[1] for o in outs],
        out_shape=[o[0] for o in outs],
        compiler_params=pltpu.CompilerParams(
            dimension_semantics=("parallel",), vmem_limit_bytes=V7X_SCOPED_VMEM_BYTES),
        name="inproj",
    )(xf, nw, w, lbl, cos_t, sin_t, wup, bgk)


def _head_masks(nh):
    lane = lax.broadcasted_iota(jnp.int32, (1, LANES), 1)
    return [(lane // (LANES // nh)) == m for m in range(nh)]


def _stack_heads(x, masks):
    if len(masks) == 1:
        return x
    return jnp.concatenate([jnp.where(m, x, 0.0) for m in masks], axis=0)


def _cumsum_rows_f32(tri, g):
    hi = g.astype(BF16)
    r1 = g - hi.astype(F32)
    mid = r1.astype(BF16)
    lo = (r1 - mid.astype(F32)).astype(BF16)
    return _mm(tri, hi) + _mm(tri, mid) + _mm(tri, lo)


def _gla_chunk_scores(qs, bs, kf_sc, b_sc, r0, masks):
    ng, nh = len(qs), len(masks)
    lane_c = lax.broadcasted_iota(jnp.int32, (SUB, CHUNK), 1)
    row_c = lax.broadcasted_iota(jnp.int32, (SUB, CHUNK), 0)
    blocks = [[[] for _ in range(nh)] for _ in range(ng)]
    for blk in range(CHUNK // SUB):
        lo = blk * SUB
        causal = (lane_c - lo) <= row_c
        dg = [[jnp.zeros((SUB, CHUNK), F32) for _ in range(nh)] for _ in range(ng)]
        for jj in range(SUB):
            j = lo + jj
            hit = lane_c == j
            for gi in range(ng):
                lanes = slice(gi * LANES, (gi + 1) * LANES)
                kj = kf_sc[r0 + j:r0 + j + 1, lanes]
                bj = b_sc[r0 + j:r0 + j + 1, lanes]
                pr = qs[gi][lo:lo + SUB] * kj * jnp.exp2(bs[gi][lo:lo + SUB] - bj)
                for m in range(nh):
                    pm = pr if nh == 1 else jnp.where(masks[m], pr, 0.0)
                    dg[gi][m] = jnp.where(hit, jnp.sum(pm, axis=-1, keepdims=True), dg[gi][m])
        for gi in range(ng):
            lanes = slice(gi * LANES, (gi + 1) * LANES)
            blk_dg = [jnp.where(causal, d, 0.0) for d in dg[gi]]
            if blk > 0:
                ref_row = b_sc[r0 + lo - 1:r0 + lo, lanes]
                qp = qs[gi][lo:lo + SUB] * jnp.exp2(bs[gi][lo:lo + SUB] - ref_row)
                kk = kf_sc[r0:r0 + lo, lanes] * jnp.exp2(ref_row - bs[gi][0:lo])
                kk = jnp.concatenate([kk, jnp.zeros((CHUNK - lo, LANES), F32)], axis=0).astype(BF16)
                lhs = _stack_heads(qp, masks)
                if lhs.shape[0] < 16:
                    lhs = jnp.concatenate([lhs, jnp.zeros((16 - lhs.shape[0], LANES), F32)], axis=0)
                off = _nt(lhs.astype(BF16), kk)
                blk_dg = [off[m * SUB:(m + 1) * SUB] + blk_dg[m] for m in range(nh)]
            for m in range(nh):
                blocks[gi][m].append(blk_dg[m])
    return [[jnp.concatenate(bl, axis=0) for bl in grp] for grp in blocks]


def _mix_gla_kernel(nh, q_ref, k_ref, v_ref, g_ref, og_ref, gn_ref, o_ref, st_ref, b_sc, kf_sc):
    @pl.when(pl.program_id(1) == 0)
    def _():
        st_ref[...] = jnp.zeros_like(st_ref)

    tb = q_ref.shape[0]
    n_groups = q_ref.shape[1] // LANES
    masks = _head_masks(nh)
    gn = gn_ref[...]
    ri = lax.broadcasted_iota(jnp.int32, (tb, tb), 0)
    ci = lax.broadcasted_iota(jnp.int32, (tb, tb), 1)
    tri = jnp.where(ri >= ci, jnp.where(ri // CHUNK == ci // CHUNK, 1.0, 0.0), 0.0).astype(BF16)
    b_sc[...] = _cumsum_rows_f32(tri, g_ref[...]) * LOG2E
    kf_sc[...] = k_ref[...].astype(F32)

    for c in range(tb // CHUNK):
        r0 = c * CHUNK
        rows = slice(r0, r0 + CHUNK)
        qs = [q_ref[rows, gi * LANES:(gi + 1) * LANES].astype(F32) for gi in range(n_groups)]
        bs = [b_sc[rows, gi * LANES:(gi + 1) * LANES] for gi in range(n_groups)]
        scores = _gla_chunk_scores(qs, bs, kf_sc, b_sc, r0, masks)
        for gi in range(n_groups):
            lanes = slice(gi * LANES, (gi + 1) * LANES)
            b = bs[gi]
            b_last = b_sc[r0 + CHUNK - 1:r0 + CHUNK, lanes]
            st = st_ref[gi]
            inter = _nt(_stack_heads(qs[gi] * jnp.exp2(b), masks).astype(BF16), st.astype(BF16))
            vs = []
            for m in range(nh):
                vl = slice((gi * nh + m) * HEAD_DV, (gi * nh + m + 1) * HEAD_DV)
                v = v_ref[rows, vl]
                vs.append(v)
                o = _mm(scores[gi][m].astype(BF16), v) + inter[m * CHUNK:(m + 1) * CHUNK]
                y = _rmsnorm(o, gn) * og_ref[rows, vl].astype(F32)
                o_ref[rows, vl] = y.astype(BF16)
            kd = _stack_heads(kf_sc[rows, lanes] * jnp.exp2(b_last - b), masks).astype(BF16)
            vcat = vs[0] if nh == 1 else jnp.concatenate(vs, axis=0)
            st_ref[gi] = st * jnp.exp2(b_last) + _tn(vcat, kd)


def _mix_ret_kernel(q_ref, k_ref, v_ref, og_ref, dmask_ref, qdec_ref, kdec_ref, sdec_ref, o_ref, st_ref):
    @pl.when(pl.program_id(1) == 0)
    def _():
        st_ref[...] = jnp.zeros_like(st_ref)

    nh = 2
    cb = q_ref.shape[0]
    masks = _head_masks(nh)
    for gi in range(q_ref.shape[1] // LANES):
        lanes = slice(gi * LANES, (gi + 1) * LANES)
        q = q_ref[:, lanes].astype(F32)
        k = k_ref[:, lanes].astype(F32)
        sc = _nt(_stack_heads(q, masks).astype(BF16), k.astype(BF16)) * dmask_ref[gi]
        st = st_ref[gi]
        inter = _nt(_stack_heads(q * qdec_ref[gi], masks).astype(BF16), st.astype(BF16))
        vs = []
        for m in range(nh):
            vl = slice((gi * nh + m) * HEAD_DV, (gi * nh + m + 1) * HEAD_DV)
            v = v_ref[:, vl]
            vs.append(v)
            o = _mm(sc[m * cb:(m + 1) * cb].astype(BF16), v) + inter[m * cb:(m + 1) * cb]
            mu = jnp.mean(o, axis=-1, keepdims=True)
            var = jnp.mean(jnp.square(o - mu), axis=-1, keepdims=True)
            y = (o - mu) * lax.rsqrt(var + NORM_EPS) * og_ref[:, vl].astype(F32)
            o_ref[:, vl] = y.astype(BF16)
        kd = _stack_heads(k * kdec_ref[gi], masks).astype(BF16)
        st_ref[gi] = st * sdec_ref[gi] + _tn(jnp.concatenate(vs, axis=0), kd)


def _mixer_call(body, name, row_inputs, const_inputs, batch, seq_len, tb, n_groups, row_scratch_width=0):
    tokens = batch * seq_len
    per_seq = seq_len // tb
    row_spec = lambda a: pl.BlockSpec((tb, a.shape[1]), lambda bi, ti: (bi * per_seq + ti, 0))
    const_spec = lambda a: pl.BlockSpec(a.shape, lambda bi, ti: (0,) * a.ndim)
    width = N_HEADS * HEAD_DV
    scratch = [pltpu.VMEM((n_groups, HEAD_DV, LANES), F32)]
    if row_scratch_width:
        scratch += [pltpu.VMEM((tb, row_scratch_width), F32)] * 2
    return pl.pallas_call(
        body,
        grid=(batch, per_seq),
        in_specs=[row_spec(a) for a in row_inputs] + [const_spec(a) for a in const_inputs],
        out_specs=pl.BlockSpec((tb, width), lambda bi, ti: (bi * per_seq + ti, 0)),
        out_shape=jax.ShapeDtypeStruct((tokens, width), BF16),
        scratch_shapes=scratch,
        compiler_params=pltpu.CompilerParams(
            dimension_semantics=("parallel", "arbitrary"), vmem_limit_bytes=V7X_SCOPED_VMEM_BYTES),
        name=name,
    )(*row_inputs, *const_inputs)


def _post_kernel(final, oa_ref, ob_ref, oc_ref, gt_ref, x_ref, wbr_ref, wout_ref, nffn_ref,
                 wg_ref, wu_ref, wd_ref, nfin_ref, out_ref):
    d = x_ref.shape[1]
    merged = None
    for i, o_ref in enumerate((oa_ref, ob_ref, oc_ref)):
        gate = _sigmoid(gt_ref[:, i * d:(i + 1) * d].astype(F32))
        term = gate * _mm(o_ref[...], wbr_ref[i])
        merged = term if merged is None else merged + term
    h = x_ref[...] + _mm(merged.astype(BF16), wout_ref[...])
    hn = _rmsnorm(h, nffn_ref[...]).astype(BF16)
    g = _mm(hn, wg_ref[...])
    u = _mm(hn, wu_ref[...])
    act = (g * _sigmoid(g) * u).astype(BF16)
    y = h + _mm(act, wd_ref[...])
    if final:
        y = _rmsnorm(y, nfin_ref[...])
    out_ref[...] = y


def _post(final, oa, ob, oc, gt, xf, wbr, wout, nffn, wg, wu, wd, nfin, tm):
    tokens, d = xf.shape
    row = lambda a: pl.BlockSpec((tm, a.shape[1]), lambda i: (i, 0))
    const = lambda a: pl.BlockSpec(a.shape, lambda i: (0,) * a.ndim, pipeline_mode=pl.Buffered(1))
    return pl.pallas_call(
        functools.partial(_post_kernel, final),
        grid=(tokens // tm,),
        in_specs=[row(oa), row(ob), row(oc), row(gt), row(xf),
                  const(wbr), const(wout), const(nffn), const(wg), const(wu), const(wd), const(nfin)],
        out_specs=pl.BlockSpec((tm, d), lambda i: (i, 0)),
        out_shape=jax.ShapeDtypeStruct((tokens, d), F32),
        compiler_params=pltpu.CompilerParams(
            dimension_semantics=("parallel",), vmem_limit_bytes=V7X_SCOPED_VMEM_BYTES),
        name="post",
    )(oa, ob, oc, gt, xf, wbr, wout, nffn, wg, wu, wd, nfin)


def _rotary_tables(seq_len):
    inv_freq = 1.0 / (ROPE_BASE ** jnp.linspace(0.0, 1.0, 32, dtype=F32))
    ang = jnp.arange(seq_len, dtype=F32)[:, None] * inv_freq[None, :]
    sin, cos = jnp.sin(ang), jnp.cos(ang)
    cos_t = jnp.tile(jnp.concatenate([cos, cos], axis=1), (1, N_HEADS))
    sin_t = jnp.tile(jnp.concatenate([-sin, sin], axis=1), (1, N_HEADS))
    return cos_t, sin_t


def _retention_tables(cb):
    log_gamma = jnp.log(1.0 - 2.0 ** (-5.0 - jnp.arange(N_HEADS, dtype=F32)))
    pos = jnp.arange(cb, dtype=F32)
    diff = pos[:, None] - pos[None, :]
    dmask = jnp.where(diff >= 0, jnp.exp(diff[None] * log_gamma[:, None, None]), 0.0)
    lane_gamma = jnp.repeat(log_gamma, 64).reshape(N_HEADS // 2, 1, LANES)
    qdec = jnp.exp((pos[None, :, None] + 1.0) * lane_gamma)
    kdec = jnp.exp((cb - 1.0 - pos[None, :, None]) * lane_gamma)
    sdec = jnp.exp(float(cb) * lane_gamma)
    return dmask.reshape(N_HEADS // 2, 2 * cb, cb), qdec, kdec, sdec


def _pack_w_in(w_in):
    sizes = (512, 512, 512, 512, 256, 256, 512, 512, 256, 256, 512, 512, GK_RANK, 3072)
    offs = np.concatenate([[0], np.cumsum(sizes)])
    seg = [w_in[:, :, offs[i]:offs[i + 1]] for i in range(len(sizes))]
    perm = np.concatenate([np.concatenate([64 * h + np.arange(0, 64, 2), 64 * h + np.arange(1, 64, 2)])
                           for h in range(N_HEADS)])
    seg[4] = seg[4][:, :, perm]
    seg[5] = seg[5][:, :, perm]
    seg[12] = jnp.pad(seg[12], ((0, 0), (0, 0), (0, LANES - GK_RANK)))
    return jnp.concatenate(seg, axis=2).astype(BF16)


def kernel(x, norm_mix, w_in, lb_logits, w_gk_up, b_gk, gn_a, gn_c, w_br_a, w_br_b, w_br_c, w_out,
           norm_ffn, w_ffn_gate, w_ffn_up, w_ffn_down, norm_final):
    batch, seq_len, d = x.shape
    tokens = batch * seq_len
    tm = min(256, seq_len)
    tb = min(256, seq_len)

    w_packed = _pack_w_in(w_in)
    assert w_packed.shape[2] == _PACKED_WIDTH
    wup = jnp.pad(w_gk_up, ((0, 0), (0, LANES - GK_RANK), (0, 0))).astype(BF16)
    wbr = jnp.stack([w_br_a, w_br_b, w_br_c], axis=1).astype(BF16)
    wout = w_out.astype(BF16)
    wg, wu, wd = w_ffn_gate.astype(BF16), w_ffn_up.astype(BF16), w_ffn_down.astype(BF16)
    cos_t, sin_t = _rotary_tables(seq_len)
    dmask, qdec, kdec, sdec = _retention_tables(tb)
    nfin = norm_final.reshape(1, d)

    xf = x.reshape(tokens, d)
    for layer in range(DEPTH):
        (qa, ka, va, lfa, oga, qb, kb, vb, ogb, qc, kc, vc, lgc, ogc, gt) = _inproj(
            layer, xf, norm_mix[layer].reshape(1, d), w_packed[layer], lb_logits, cos_t, sin_t,
            wup[layer], b_gk[layer].reshape(1, -1), seq_len, tm)
        oa = _mixer_call(functools.partial(_mix_gla_kernel, 1), "mix_a",
                         [qa, ka, va, lfa, oga], [gn_a[layer].reshape(1, -1)], batch, seq_len, tb, 4,
                         row_scratch_width=512)
        ob = _mixer_call(_mix_ret_kernel, "mix_b",
                         [qb, kb, vb, ogb], [dmask, qdec, kdec, sdec], batch, seq_len, tb, 2)
        oc = _mixer_call(functools.partial(_mix_gla_kernel, 2), "mix_c",
                         [qc, kc, vc, lgc, ogc], [gn_c[layer].reshape(1, -1)], batch, seq_len, tb, 2,
                         row_scratch_width=256)
        xf = _post(layer == DEPTH - 1, oa, ob, oc, gt, xf, wbr[layer], wout[layer],
                   norm_ffn[layer].reshape(1, d), wg[layer], wu[layer], wd[layer], nfin, tm)
    return xf.reshape(batch, seq_len, d)
```

```python
import functools

import numpy as np
import jax
import jax.numpy as jnp
from jax import lax
from jax.experimental import pallas as pl
from jax.experimental.pallas import tpu as pltpu

F32 = jnp.float32
BF16 = jnp.bfloat16

DEPTH = 4
CHUNK = 64
SUB = 8
NORM_EPS = 1e-6
N_HEADS = 4
HEAD_DV = 128
ROPE_BASE = 10000.0
GK_RANK = 16
GK_NORMALIZER = 16.0
LOG2E = 1.4426950408889634
LANES = 128
V7X_SCOPED_VMEM_BYTES = 60000 * 1024

_A_Q, _A_F, _A_I, _A_G = 0, 512, 1024, 1536
_B_Q, _B_K, _B_V, _B_G = 2048, 2304, 2560, 3072
_C_Q, _C_K, _C_V, _C_G = 3584, 3840, 4096, 4608
_C_R = 5120
_GATES = 5248
_PACKED_WIDTH = 8320


def _nt(a, b):
    return lax.dot_general(a, b, (((1,), (1,)), ((), ())), preferred_element_type=F32)


def _tn(a, b):
    return lax.dot_general(a, b, (((0,), (0,)), ((), ())), preferred_element_type=F32)


def _mm(a, b):
    return jnp.dot(a, b, preferred_element_type=F32)


def _sigmoid(z):
    return 1.0 / (1.0 + jnp.exp(-z))


def _log_sigmoid(z):
    return jnp.minimum(z, 0.0) - jnp.log1p(jnp.exp(-jnp.abs(z)))


def _rmsnorm(x, w):
    ms = jnp.mean(x * x, axis=-1, keepdims=True)
    return x * lax.rsqrt(ms + NORM_EPS) * w


def _swap_rot_halves(t):
    n = t.shape[1]
    lane = lax.broadcasted_iota(jnp.int32, t.shape, 1)
    first = (lane % 64) < 32
    return jnp.where(first, pltpu.roll(t, n - 32, 1), pltpu.roll(t, 32, 1))


def _inproj_kernel(layer, x_ref, nw_ref, w_ref, lbl_ref, cos_ref, sin_ref, wup_ref, bgk_ref,
                   qa_ref, ka_ref, va_ref, lfa_ref, oga_ref,
                   qb_ref, kb_ref, vb_ref, ogb_ref,
                   qc_ref, kc_ref, vc_ref, lgc_ref, ogc_ref, gt_ref):
    xn = _rmsnorm(x_ref[...], nw_ref[...]).astype(BF16)

    def proj(lo, hi):
        return _mm(xn, w_ref[:, lo:hi])

    lbl = lbl_ref[...]
    e = jnp.exp(lbl - jnp.max(lbl, axis=0, keepdims=True))
    p = e / jnp.sum(e, axis=0, keepdims=True)
    c = p[0:1]
    for r in range(1, layer + 1):
        c = c + p[r:r + 1]
    lb = c - p[0:1]

    aq = proj(_A_Q, _A_F)
    qa_ref[...] = (aq * _sigmoid(aq) * (HEAD_DV ** -0.5)).astype(BF16)
    z = proj(_A_F, _A_I)
    ls = jnp.log1p(-lb) + _log_sigmoid(z)
    la = jnp.log(jnp.where(lb > 0.0, lb, 1.0))
    both = jnp.maximum(la, ls) + jnp.log1p(jnp.exp(-jnp.abs(la - ls)))
    lfa_ref[...] = jnp.where(lb > 0.0, both, ls)
    ka_ref[...] = ((1.0 - lb) * _sigmoid(-z)).astype(BF16)
    va_ref[...] = proj(_A_I, _A_G).astype(BF16)
    ga = proj(_A_G, _B_Q)
    oga_ref[...] = (ga * _sigmoid(ga)).astype(BF16)

    cos = cos_ref[...]
    sin = sin_ref[...]
    qb = proj(_B_Q, _B_K)
    qb_ref[...] = (qb * cos + _swap_rot_halves(qb) * sin).astype(BF16)
    kb = proj(_B_K, _B_V)
    kb_ref[...] = ((kb * cos + _swap_rot_halves(kb) * sin) * (64 ** -0.5)).astype(BF16)
    vb_ref[...] = proj(_B_V, _B_G).astype(BF16)
    gb = proj(_B_G, _C_Q)
    ogb_ref[...] = (gb * _sigmoid(gb)).astype(BF16)

    qc_ref[...] = (proj(_C_Q, _C_K) * (64 ** -0.5)).astype(BF16)
    kc_ref[...] = proj(_C_K, _C_V).astype(BF16)
    vc_ref[...] = proj(_C_V, _C_G).astype(BF16)
    gc = proj(_C_G, _C_R)
    ogc_ref[...] = (gc * _sigmoid(gc)).astype(BF16)
    code = proj(_C_R, _GATES).astype(BF16)
    gk = _mm(code, wup_ref[...]) + bgk_ref[...]
    lgc_ref[...] = _log_sigmoid(gk) * (1.0 / GK_NORMALIZER)

    gt_ref[...] = proj(_GATES, _PACKED_WIDTH).astype(BF16)


def _inproj(layer, xf, nw, w, lbl, cos_t, sin_t, wup, bgk, seq_len, tm):
    tokens, d = xf.shape
    nt = tokens // tm
    per_seq = seq_len // tm

    def rows(width, dtype):
        return jax.ShapeDtypeStruct((tokens, width), dtype), pl.BlockSpec((tm, width), lambda i: (i, 0))

    outs = [rows(512, BF16), rows(512, BF16), rows(512, BF16), rows(512, F32), rows(512, BF16),
            rows(256, BF16), rows(256, BF16), rows(512, BF16), rows(512, BF16),
            rows(256, BF16), rows(256, BF16), rows(512, BF16), rows(256, F32), rows(512, BF16),
            rows(3072, BF16)]
    const = lambda shape: pl.BlockSpec(shape, lambda i: (0,) * len(shape), pipeline_mode=pl.Buffered(1))
    in_specs = [
        pl.BlockSpec((tm, d), lambda i: (i, 0)),
        const((1, d)),
        const(w.shape),
        const(lbl.shape),
        pl.BlockSpec((tm, 256), lambda i: (i % per_seq, 0)),
        pl.BlockSpec((tm, 256), lambda i: (i % per_seq, 0)),
        const(wup.shape),
        const(bgk.shape),
    ]
    return pl.pallas_call(
        functools.partial(_inproj_kernel, layer),
        grid=(nt,),
        in_specs=in_specs,
        out_specs=[o[1] for o in outs],
        out_shape=[o[0] for o in outs],
        compiler_params=pltpu.CompilerParams(
            dimension_semantics=("parallel",), vmem_limit_bytes=V7X_SCOPED_VMEM_BYTES),
        name="inproj",
    )(xf, nw, w, lbl, cos_t, sin_t, wup, bgk)


def _head_masks(nh):
    lane = lax.broadcasted_iota(jnp.int32, (1, LANES), 1)
    return [(lane // (LANES // nh)) == m for m in range(nh)]


def _stack_heads(x, masks):
    if len(masks) == 1:
        return x
    return jnp.concatenate([jnp.where(m, x, 0.0) for m in masks], axis=0)


def _cumsum_rows_f32(tri, g):
    hi = g.astype(BF16)
    r1 = g - hi.astype(F32)
    mid = r1.astype(BF16)
    lo = (r1 - mid.astype(F32)).astype(BF16)
    return _mm(tri, hi) + _mm(tri, mid) + _mm(tri, lo)


def _gla_chunk_scores(qs, bs, kf_sc, b_sc, r0, masks):
    ng, nh = len(qs), len(masks)
    lane_c = lax.broadcasted_iota(jnp.int32, (SUB, CHUNK), 1)
    row_c = lax.broadcasted_iota(jnp.int32, (SUB, CHUNK), 0)
    blocks = [[[] for _ in range(nh)] for _ in range(ng)]
    for blk in range(CHUNK // SUB):
        lo = blk * SUB
        causal = (lane_c - lo) <= row_c
        dg = [[jnp.zeros((SUB, CHUNK), F32) for _ in range(nh)] for _ in range(ng)]
        for jj in range(SUB):
            j = lo + jj
            hit = lane_c == j
            for gi in range(ng):
                lanes = slice(gi * LANES, (gi + 1) * LANES)
                kj = kf_sc[r0 + j:r0 + j + 1, lanes]
                bj = b_sc[r0 + j:r0 + j + 1, lanes]
                pr = qs[gi][lo:lo + SUB] * kj * jnp.exp2(bs[gi][lo:lo + SUB] - bj)
                for m in range(nh):
                    pm = pr if nh == 1 else jnp.where(masks[m], pr, 0.0)
                    dg[gi][m] = jnp.where(hit, jnp.sum(pm, axis=-1, keepdims=True), dg[gi][m])
        for gi in range(ng):
            lanes = slice(gi * LANES, (gi + 1) * LANES)
            blk_dg = [jnp.where(causal, d, 0.0) for d in dg[gi]]
            if blk > 0:
                ref_row = b_sc[r0 + lo - 1:r0 + lo, lanes]
                qp = qs[gi][lo:lo + SUB] * jnp.exp2(bs[gi][lo:lo + SUB] - ref_row)
                kk = kf_sc[r0:r0 + lo, lanes] * jnp.exp2(ref_row - bs[gi][0:lo])
                kk = jnp.concatenate([kk, jnp.zeros((CHUNK - lo, LANES), F32)], axis=0).astype(BF16)
                lhs = _stack_heads(qp, masks)
                if lhs.shape[0] < 16:
                    lhs = jnp.concatenate([lhs, jnp.zeros((16 - lhs.shape[0], LANES), F32)], axis=0)
                off = _nt(lhs.astype(BF16), kk)
                blk_dg = [off[m * SUB:(m + 1) * SUB] + blk_dg[m] for m in range(nh)]
            for m in range(nh):
                blocks[gi][m].append(blk_dg[m])
    return [[jnp.concatenate(bl, axis=0) for bl in grp] for grp in blocks]


def _gla_stage(nh, q_ref, k_ref, v_ref, g_ref, og_ref, gn_ref, o_ref, o_base, st_ref, b_sc, kf_sc):
    tb = q_ref.shape[0]
    n_groups = q_ref.shape[1] // LANES
    masks = _head_masks(nh)
    gn = gn_ref[...]
    ri = lax.broadcasted_iota(jnp.int32, (tb, tb), 0)
    ci = lax.broadcasted_iota(jnp.int32, (tb, tb), 1)
    tri = jnp.where(ri >= ci, jnp.where(ri // CHUNK == ci // CHUNK, 1.0, 0.0), 0.0).astype(BF16)
    w = q_ref.shape[1]
    b_sc[:, 0:w] = _cumsum_rows_f32(tri, g_ref[...]) * LOG2E
    kf_sc[:, 0:w] = k_ref[...].astype(F32)

    for c in range(tb // CHUNK):
        r0 = c * CHUNK
        rows = slice(r0, r0 + CHUNK)
        qs = [q_ref[rows, gi * LANES:(gi + 1) * LANES].astype(F32) for gi in range(n_groups)]
        bs = [b_sc[rows, gi * LANES:(gi + 1) * LANES] for gi in range(n_groups)]
        scores = _gla_chunk_scores(qs, bs, kf_sc, b_sc, r0, masks)
        for gi in range(n_groups):
            lanes = slice(gi * LANES, (gi + 1) * LANES)
            b = bs[gi]
            b_last = b_sc[r0 + CHUNK - 1:r0 + CHUNK, lanes]
            st = st_ref[gi]
            inter = _nt(_stack_heads(qs[gi] * jnp.exp2(b), masks).astype(BF16), st.astype(BF16))
            vs = []
            for m in range(nh):
                vl = slice((gi * nh + m) * HEAD_DV, (gi * nh + m + 1) * HEAD_DV)
                v = v_ref[rows, vl]
                vs.append(v)
                o = _mm(scores[gi][m].astype(BF16), v) + inter[m * CHUNK:(m + 1) * CHUNK]
                y = _rmsnorm(o, gn) * og_ref[rows, vl].astype(F32)
                o_ref[rows, o_base + vl.start:o_base + vl.stop] = y.astype(BF16)
            kd = _stack_heads(kf_sc[rows, lanes] * jnp.exp2(b_last - b), masks).astype(BF16)
            vcat = vs[0] if nh == 1 else jnp.concatenate(vs, axis=0)
            st_ref[gi] = st * jnp.exp2(b_last) + _tn(vcat, kd)


def _ret_stage(q_ref, k_ref, v_ref, og_ref, dmask_ref, qdec_ref, kdec_ref, sdec_ref, o_ref, o_base, st_ref,
               kf_sc):
    nh = 2
    cb = q_ref.shape[0]
    masks = _head_masks(nh)
    kf_sc[:, 0:q_ref.shape[1]] = k_ref[...].astype(F32)
    for gi in range(q_ref.shape[1] // LANES):
        lanes = slice(gi * LANES, (gi + 1) * LANES)
        q = q_ref[:, lanes].astype(F32)
        k = kf_sc[:, lanes]
        sc = _nt(_stack_heads(q, masks).astype(BF16), k_ref[:, lanes]) * dmask_ref[gi]
        st = st_ref[gi]
        inter = _nt(_stack_heads(q * qdec_ref[gi], masks).astype(BF16), st.astype(BF16))
        vs = []
        for m in range(nh):
            vl = slice((gi * nh + m) * HEAD_DV, (gi * nh + m + 1) * HEAD_DV)
            v = v_ref[:, vl]
            vs.append(v)
            o = _mm(sc[m * cb:(m + 1) * cb].astype(BF16), v) + inter[m * cb:(m + 1) * cb]
            mu = jnp.mean(o, axis=-1, keepdims=True)
            var = jnp.mean(jnp.square(o - mu), axis=-1, keepdims=True)
            y = (o - mu) * lax.rsqrt(var + NORM_EPS) * og_ref[:, vl].astype(F32)
            o_ref[:, o_base + vl.start:o_base + vl.stop] = y.astype(BF16)
        kd = _stack_heads(k * kdec_ref[gi], masks).astype(BF16)
        st_ref[gi] = st * sdec_ref[gi] + _tn(jnp.concatenate(vs, axis=0), kd)


def _post_stage(final, o_branches, gt_ref, x_ref, wbr_ref, wout_ref, nffn_ref, wg_ref, wu_ref, wd_ref,
                nfin_ref, out_ref):
    d = x_ref.shape[1]
    merged = None
    for i, o_branch in enumerate(o_branches):
        gate = _sigmoid(gt_ref[:, i * d:(i + 1) * d].astype(F32))
        term = gate * _mm(o_branch, wbr_ref[i])
        merged = term if merged is None else merged + term
    h = x_ref[...] + _mm(merged.astype(BF16), wout_ref[...])
    hn = _rmsnorm(h, nffn_ref[...]).astype(BF16)
    g = _mm(hn, wg_ref[...])
    u = _mm(hn, wu_ref[...])
    y = h + _mm((g * _sigmoid(g) * u).astype(BF16), wd_ref[...])
    if final:
        y = _rmsnorm(y, nfin_ref[...])
    out_ref[...] = y


def _mixpost_kernel(final, per_seq,
                    qa_ref, ka_ref, va_ref, lfa_ref, oga_ref, qb_ref, kb_ref, vb_ref, ogb_ref,
                    qc_ref, kc_ref, vc_ref, lgc_ref, ogc_ref, gt_ref, x_ref,
                    gna_ref, gnc_ref, dmask_ref, qdec_ref, kdec_ref, sdec_ref,
                    wbr_ref, wout_ref, nffn_ref, wg_ref, wu_ref, wd_ref, nfin_ref,
                    out_ref,
                    sta_ref, stb_ref, stc_ref, b_sc, kf_sc, o_sc):
    s = pl.program_id(0)

    @pl.when(s % per_seq == 0)
    def _():
        sta_ref[...] = jnp.zeros_like(sta_ref)
        stb_ref[...] = jnp.zeros_like(stb_ref)
        stc_ref[...] = jnp.zeros_like(stc_ref)

    @pl.when(s == 0)
    def _():
        o_sc[...] = jnp.zeros_like(o_sc)

    width = N_HEADS * HEAD_DV
    o_prev = [o_sc[:, i * width:(i + 1) * width] for i in range(3)]
    _gla_stage(1, qa_ref, ka_ref, va_ref, lfa_ref, oga_ref, gna_ref, o_sc, 0, sta_ref, b_sc, kf_sc)
    _ret_stage(qb_ref, kb_ref, vb_ref, ogb_ref, dmask_ref, qdec_ref, kdec_ref, sdec_ref, o_sc, width,
               stb_ref, kf_sc)
    _gla_stage(2, qc_ref, kc_ref, vc_ref, lgc_ref, ogc_ref, gnc_ref, o_sc, 2 * width, stc_ref, b_sc,
               kf_sc)
    _post_stage(final, o_prev, gt_ref, x_ref, wbr_ref, wout_ref, nffn_ref, wg_ref, wu_ref, wd_ref,
                nfin_ref, out_ref)


def _mixpost(final, mix_rows, gt, xf, consts, seq_len, tb):
    tokens, d = xf.shape
    n_blocks = tokens // tb
    per_seq = seq_len // tb
    width = N_HEADS * HEAD_DV
    cur = lambda a: pl.BlockSpec((tb, a.shape[1]), lambda s: (jnp.minimum(s, n_blocks - 1), 0))
    prev = lambda a: pl.BlockSpec((tb, a.shape[1]), lambda s: (jnp.maximum(s - 1, 0), 0))
    const = lambda a: pl.BlockSpec(a.shape, lambda s: (0,) * a.ndim, pipeline_mode=pl.Buffered(1))
    state = lambda n: pltpu.VMEM((n, HEAD_DV, LANES), F32)
    rows_f32 = lambda w: pltpu.VMEM((tb, w), F32)
    return pl.pallas_call(
        functools.partial(_mixpost_kernel, final, per_seq),
        grid=(n_blocks + 1,),
        in_specs=[cur(a) for a in mix_rows] + [prev(gt), prev(xf)] + [const(a) for a in consts],
        out_specs=pl.BlockSpec((tb, d), lambda s: (jnp.maximum(s - 1, 0), 0)),
        out_shape=jax.ShapeDtypeStruct((tokens, d), F32),
        scratch_shapes=[state(4), state(2), state(2), rows_f32(512), rows_f32(512),
                        pltpu.VMEM((tb, 3 * width), BF16)],
        compiler_params=pltpu.CompilerParams(
            dimension_semantics=("arbitrary",), vmem_limit_bytes=V7X_SCOPED_VMEM_BYTES),
        name="mixpost",
    )(*mix_rows, gt, xf, *consts)


def _rotary_tables(seq_len):
    inv_freq = 1.0 / (ROPE_BASE ** jnp.linspace(0.0, 1.0, 32, dtype=F32))
    ang = jnp.arange(seq_len, dtype=F32)[:, None] * inv_freq[None, :]
    sin, cos = jnp.sin(ang), jnp.cos(ang)
    cos_t = jnp.tile(jnp.concatenate([cos, cos], axis=1), (1, N_HEADS))
    sin_t = jnp.tile(jnp.concatenate([-sin, sin], axis=1), (1, N_HEADS))
    return cos_t, sin_t


def _retention_tables(cb):
    log_gamma = jnp.log(1.0 - 2.0 ** (-5.0 - jnp.arange(N_HEADS, dtype=F32)))
    pos = jnp.arange(cb, dtype=F32)
    diff = pos[:, None] - pos[None, :]
    dmask = jnp.where(diff >= 0, jnp.exp(diff[None] * log_gamma[:, None, None]), 0.0)
    lane_gamma = jnp.repeat(log_gamma, 64).reshape(N_HEADS // 2, 1, LANES)
    qdec = jnp.exp((pos[None, :, None] + 1.0) * lane_gamma)
    kdec = jnp.exp((cb - 1.0 - pos[None, :, None]) * lane_gamma)
    sdec = jnp.exp(float(cb) * lane_gamma)
    return dmask.reshape(N_HEADS // 2, 2 * cb, cb), qdec, kdec, sdec


def _pack_w_in(w_in):
    sizes = (512, 512, 512, 512, 256, 256, 512, 512, 256, 256, 512, 512, GK_RANK, 3072)
    offs = np.concatenate([[0], np.cumsum(sizes)])
    seg = [w_in[:, :, offs[i]:offs[i + 1]] for i in range(len(sizes))]
    perm = np.concatenate([np.concatenate([64 * h + np.arange(0, 64, 2), 64 * h + np.arange(1, 64, 2)])
                           for h in range(N_HEADS)])
    seg[4] = seg[4][:, :, perm]
    seg[5] = seg[5][:, :, perm]
    seg[12] = jnp.pad(seg[12], ((0, 0), (0, 0), (0, LANES - GK_RANK)))
    return jnp.concatenate(seg, axis=2).astype(BF16)


def kernel(x, norm_mix, w_in, lb_logits, w_gk_up, b_gk, gn_a, gn_c, w_br_a, w_br_b, w_br_c, w_out,
           norm_ffn, w_ffn_gate, w_ffn_up, w_ffn_down, norm_final):
    batch, seq_len, d = x.shape
    tokens = batch * seq_len
    tm = min(256, seq_len)
    tb = min(256, seq_len)

    w_packed = _pack_w_in(w_in)
    assert w_packed.shape[2] == _PACKED_WIDTH
    wup = jnp.pad(w_gk_up, ((0, 0), (0, LANES - GK_RANK), (0, 0))).astype(BF16)
    wbr = jnp.stack([w_br_a, w_br_b, w_br_c], axis=1).astype(BF16)
    wout = w_out.astype(BF16)
    wg, wu, wd = w_ffn_gate.astype(BF16), w_ffn_up.astype(BF16), w_ffn_down.astype(BF16)
    cos_t, sin_t = _rotary_tables(seq_len)
    dmask, qdec, kdec, sdec = _retention_tables(tb)
    nfin = norm_final.reshape(1, d)

    xf = x.reshape(tokens, d)
    for layer in range(DEPTH):
        (qa, ka, va, lfa, oga, qb, kb, vb, ogb, qc, kc, vc, lgc, ogc, gt) = _inproj(
            layer, xf, norm_mix[layer].reshape(1, d), w_packed[layer], lb_logits, cos_t, sin_t,
            wup[layer], b_gk[layer].reshape(1, -1), seq_len, tm)
        consts = [gn_a[layer].reshape(1, -1), gn_c[layer].reshape(1, -1), dmask, qdec, kdec, sdec,
                  wbr[layer], wout[layer], norm_ffn[layer].reshape(1, d), wg[layer], wu[layer], wd[layer],
                  nfin]
        xf = _mixpost(layer == DEPTH - 1, [qa, ka, va, lfa, oga, qb, kb, vb, ogb, qc, kc, vc, lgc, ogc],
                      gt, xf, consts, seq_len, tb)
    return xf.reshape(batch, seq_len, d)
```

```python
import functools

import jax
import jax.numpy as jnp
from jax import lax
from jax.experimental import pallas as pl
from jax.experimental.pallas import tpu as pltpu

F32 = jnp.float32
BF16 = jnp.bfloat16

DEPTH = 4
CHUNK = 64
SUB = 8
NORM_EPS = 1e-6
N_HEADS = 4
HEAD_DV = 128
ROPE_BASE = 10000.0
GK_RANK = 16
GK_NORMALIZER = 16.0
LOG2E = 1.4426950408889634
LANES = 128
V7X_SCOPED_VMEM_BYTES = 60000 * 1024

_A_Q, _A_F, _A_I, _A_G = 0, 512, 1024, 1536
_B_Q, _B_K, _B_V, _B_G = 2048, 2304, 2560, 3072
_C_Q, _C_K, _C_V, _C_G = 3584, 3840, 4096, 4608
_C_R = 5120
_GATES = _C_R + GK_RANK


def _nt(a, b):
    return lax.dot_general(a, b, (((1,), (1,)), ((), ())), preferred_element_type=F32)


def _tn(a, b):
    return lax.dot_general(a, b, (((0,), (0,)), ((), ())), preferred_element_type=F32)


def _mm(a, b):
    return jnp.dot(a, b, preferred_element_type=F32)


def _sigmoid(z):
    return 1.0 / (1.0 + jnp.exp(-z))


def _log_sigmoid(z):
    return jnp.minimum(z, 0.0) - jnp.log1p(jnp.exp(-jnp.abs(z)))


def _rmsnorm(x, w):
    ms = jnp.mean(x * x, axis=-1, keepdims=True)
    return x * lax.rsqrt(ms + NORM_EPS) * w


def _swap_lane_pairs(t):
    n = t.shape[1]
    lane = lax.broadcasted_iota(jnp.int32, t.shape, 1)
    return jnp.where(lane % 2 == 0, pltpu.roll(t, n - 1, 1), pltpu.roll(t, 1, 1))


def _inproj_kernel(layer, x_ref, nw_ref, w_ref, wr_ref, wgt_ref, lbl_ref, cos_ref, sin_ref, wup_ref, bgk_ref,
                   qa_ref, ka_ref, va_ref, lfa_ref, oga_ref,
                   qb_ref, kb_ref, vb_ref, ogb_ref,
                   qc_ref, kc_ref, vc_ref, lgc_ref, ogc_ref, gt_ref):
    xn = _rmsnorm(x_ref[...], nw_ref[...]).astype(BF16)

    def proj(lo, hi):
        return _mm(xn, w_ref[:, lo:hi])

    lbl = lbl_ref[...]
    e = jnp.exp(lbl - jnp.max(lbl, axis=0, keepdims=True))
    p = e / jnp.sum(e, axis=0, keepdims=True)
    c = p[0:1]
    for r in range(1, layer + 1):
        c = c + p[r:r + 1]
    lb = c - p[0:1]

    aq = proj(_A_Q, _A_F)
    qa_ref[...] = (aq * _sigmoid(aq) * (HEAD_DV ** -0.5)).astype(BF16)
    z = proj(_A_F, _A_I)
    ls = jnp.log1p(-lb) + _log_sigmoid(z)
    la = jnp.log(jnp.where(lb > 0.0, lb, 1.0))
    both = jnp.maximum(la, ls) + jnp.log1p(jnp.exp(-jnp.abs(la - ls)))
    lfa_ref[...] = jnp.where(lb > 0.0, both, ls)
    ka_ref[...] = ((1.0 - lb) * _sigmoid(-z)).astype(BF16)
    va_ref[...] = proj(_A_I, _A_G).astype(BF16)
    ga = proj(_A_G, _B_Q)
    oga_ref[...] = (ga * _sigmoid(ga)).astype(BF16)

    cos = cos_ref[...]
    sin = sin_ref[...]
    qb = proj(_B_Q, _B_K)
    qb_ref[...] = (qb * cos + _swap_lane_pairs(qb) * sin).astype(BF16)
    kb = proj(_B_K, _B_V)
    kb_ref[...] = ((kb * cos + _swap_lane_pairs(kb) * sin) * (64 ** -0.5)).astype(BF16)
    vb_ref[...] = proj(_B_V, _B_G).astype(BF16)
    gb = proj(_B_G, _C_Q)
    ogb_ref[...] = (gb * _sigmoid(gb)).astype(BF16)

    qc_ref[...] = (proj(_C_Q, _C_K) * (64 ** -0.5)).astype(BF16)
    kc_ref[...] = proj(_C_K, _C_V).astype(BF16)
    vc_ref[...] = proj(_C_V, _C_G).astype(BF16)
    gc = proj(_C_G, _C_R)
    ogc_ref[...] = (gc * _sigmoid(gc)).astype(BF16)
    code = _mm(xn, wr_ref[...]).astype(BF16)
    gk = _mm(code, wup_ref[...]) + bgk_ref[...]
    lgc_ref[...] = _log_sigmoid(gk) * (1.0 / GK_NORMALIZER)

    gt_ref[...] = _mm(xn, wgt_ref[...]).astype(BF16)


def _inproj(layer, xf, nw, w, wr, wgt, lbl, cos_t, sin_t, wup, bgk, seq_len, tm):
    tokens, d = xf.shape
    nt = tokens // tm
    per_seq = seq_len // tm

    def rows(width, dtype):
        return jax.ShapeDtypeStruct((tokens, width), dtype), pl.BlockSpec((tm, width), lambda i: (i, 0))

    outs = [rows(512, BF16), rows(512, BF16), rows(512, BF16), rows(512, F32), rows(512, BF16),
            rows(256, BF16), rows(256, BF16), rows(512, BF16), rows(512, BF16),
            rows(256, BF16), rows(256, BF16), rows(512, BF16), rows(256, F32), rows(512, BF16),
            rows(3072, BF16)]
    const = lambda shape: pl.BlockSpec(shape, lambda i: (0,) * len(shape), pipeline_mode=pl.Buffered(1))
    in_specs = [
        pl.BlockSpec((tm, d), lambda i: (i, 0)),
        const((1, d)),
        const(w.shape),
        const(wr.shape),
        const(wgt.shape),
        const(lbl.shape),
        pl.BlockSpec((tm, 256), lambda i: (i % per_seq, 0)),
        pl.BlockSpec((tm, 256), lambda i: (i % per_seq, 0)),
        const(wup.shape),
        const(bgk.shape),
    ]
    return pl.pallas_call(
        functools.partial(_inproj_kernel, layer),
        grid=(nt,),
        in_specs=in_specs,
        out_specs=[o[1] for o in outs],
        out_shape=[o[0] for o in outs],
        compiler_params=pltpu.CompilerParams(
            dimension_semantics=("parallel",), vmem_limit_bytes=V7X_SCOPED_VMEM_BYTES),
        name="inproj",
    )(xf, nw, w, wr, wgt, lbl, cos_t, sin_t, wup, bgk)


def _head_masks(nh):
    lane = lax.broadcasted_iota(jnp.int32, (1, LANES), 1)
    return [(lane // (LANES // nh)) == m for m in range(nh)]


def _stack_heads(x, masks):
    if len(masks) == 1:
        return x
    return jnp.concatenate([jnp.where(m, x, 0.0) for m in masks], axis=0)


def _cumsum_rows_f32(tri, g):
    hi = g.astype(BF16)
    r1 = g - hi.astype(F32)
    mid = r1.astype(BF16)
    lo = (r1 - mid.astype(F32)).astype(BF16)
    return _mm(tri, hi) + _mm(tri, mid) + _mm(tri, lo)


def _gla_chunk_scores(qs, bs, kf_sc, b_sc, r0, masks):
    ng, nh = len(qs), len(masks)
    lane_c = lax.broadcasted_iota(jnp.int32, (SUB, CHUNK), 1)
    row_c = lax.broadcasted_iota(jnp.int32, (SUB, CHUNK), 0)
    blocks = [[[] for _ in range(nh)] for _ in range(ng)]
    for blk in range(CHUNK // SUB):
        lo = blk * SUB
        causal = (lane_c - lo) <= row_c
        dg = [[jnp.zeros((SUB, CHUNK), F32) for _ in range(nh)] for _ in range(ng)]
        for jj in range(SUB):
            j = lo + jj
            hit = lane_c == j
            for gi in range(ng):
                lanes = slice(gi * LANES, (gi + 1) * LANES)
                kj = kf_sc[r0 + j:r0 + j + 1, lanes]
                bj = b_sc[r0 + j:r0 + j + 1, lanes]
                pr = qs[gi][lo:lo + SUB] * kj * jnp.exp2(bs[gi][lo:lo + SUB] - bj)
                for m in range(nh):
                    pm = pr if nh == 1 else jnp.where(masks[m], pr, 0.0)
                    dg[gi][m] = jnp.where(hit, jnp.sum(pm, axis=-1, keepdims=True), dg[gi][m])
        for gi in range(ng):
            lanes = slice(gi * LANES, (gi + 1) * LANES)
            blk_dg = [jnp.where(causal, d, 0.0) for d in dg[gi]]
            if blk > 0:
                ref_row = b_sc[r0 + lo - 1:r0 + lo, lanes]
                qp = qs[gi][lo:lo + SUB] * jnp.exp2(bs[gi][lo:lo + SUB] - ref_row)
                kk = kf_sc[r0:r0 + lo, lanes] * jnp.exp2(ref_row - bs[gi][0:lo])
                kk = jnp.concatenate([kk, jnp.zeros((CHUNK - lo, LANES), F32)], axis=0).astype(BF16)
                lhs = _stack_heads(qp, masks)
                if lhs.shape[0] < 16:
                    lhs = jnp.concatenate([lhs, jnp.zeros((16 - lhs.shape[0], LANES), F32)], axis=0)
                off = _nt(lhs.astype(BF16), kk)
                blk_dg = [off[m * SUB:(m + 1) * SUB] + blk_dg[m] for m in range(nh)]
            for m in range(nh):
                blocks[gi][m].append(blk_dg[m])
    return [[jnp.concatenate(bl, axis=0) for bl in grp] for grp in blocks]


def _gla_stage(nh, q_ref, k_ref, v_ref, g_ref, og_ref, gn_ref, o_ref, o_base, st_ref, b_sc, kf_sc):
    tb = q_ref.shape[0]
    n_groups = q_ref.shape[1] // LANES
    masks = _head_masks(nh)
    gn = gn_ref[...]
    ri = lax.broadcasted_iota(jnp.int32, (tb, tb), 0)
    ci = lax.broadcasted_iota(jnp.int32, (tb, tb), 1)
    tri = jnp.where(ri >= ci, jnp.where(ri // CHUNK == ci // CHUNK, 1.0, 0.0), 0.0).astype(BF16)
    w = q_ref.shape[1]
    b_sc[:, 0:w] = _cumsum_rows_f32(tri, g_ref[...]) * LOG2E
    kf_sc[:, 0:w] = k_ref[...].astype(F32)

    for c in range(tb // CHUNK):
        r0 = c * CHUNK
        rows = slice(r0, r0 + CHUNK)
        qs = [q_ref[rows, gi * LANES:(gi + 1) * LANES].astype(F32) for gi in range(n_groups)]
        bs = [b_sc[rows, gi * LANES:(gi + 1) * LANES] for gi in range(n_groups)]
        scores = _gla_chunk_scores(qs, bs, kf_sc, b_sc, r0, masks)
        for gi in range(n_groups):
            lanes = slice(gi * LANES, (gi + 1) * LANES)
            b = bs[gi]
            b_last = b_sc[r0 + CHUNK - 1:r0 + CHUNK, lanes]
            st = st_ref[gi]
            inter = _nt(_stack_heads(qs[gi] * jnp.exp2(b), masks).astype(BF16), st.astype(BF16))
            vs = []
            for m in range(nh):
                vl = slice((gi * nh + m) * HEAD_DV, (gi * nh + m + 1) * HEAD_DV)
                v = v_ref[rows, vl]
                vs.append(v)
                o = _mm(scores[gi][m].astype(BF16), v) + inter[m * CHUNK:(m + 1) * CHUNK]
                y = _rmsnorm(o, gn) * og_ref[rows, vl].astype(F32)
                o_ref[rows, o_base + vl.start:o_base + vl.stop] = y.astype(BF16)
            kd = _stack_heads(kf_sc[rows, lanes] * jnp.exp2(b_last - b), masks).astype(BF16)
            vcat = vs[0] if nh == 1 else jnp.concatenate(vs, axis=0)
            st_ref[gi] = st * jnp.exp2(b_last) + _tn(vcat, kd)


def _ret_stage(q_ref, k_ref, v_ref, og_ref, dmask_ref, qdec_ref, kdec_ref, sdec_ref, o_ref, o_base, st_ref,
               kf_sc):
    nh = 2
    cb = q_ref.shape[0]
    masks = _head_masks(nh)
    kf_sc[:, 0:q_ref.shape[1]] = k_ref[...].astype(F32)
    for gi in range(q_ref.shape[1] // LANES):
        lanes = slice(gi * LANES, (gi + 1) * LANES)
        q = q_ref[:, lanes].astype(F32)
        k = kf_sc[:, lanes]
        sc = _nt(_stack_heads(q, masks).astype(BF16), k_ref[:, lanes]) * dmask_ref[gi]
        st = st_ref[gi]
        inter = _nt(_stack_heads(q * qdec_ref[gi], masks).astype(BF16), st.astype(BF16))
        vs = []
        for m in range(nh):
            vl = slice((gi * nh + m) * HEAD_DV, (gi * nh + m + 1) * HEAD_DV)
            v = v_ref[:, vl]
            vs.append(v)
            o = _mm(sc[m * cb:(m + 1) * cb].astype(BF16), v) + inter[m * cb:(m + 1) * cb]
            mu = jnp.mean(o, axis=-1, keepdims=True)
            var = jnp.mean(jnp.square(o - mu), axis=-1, keepdims=True)
            y = (o - mu) * lax.rsqrt(var + NORM_EPS) * og_ref[:, vl].astype(F32)
            o_ref[:, o_base + vl.start:o_base + vl.stop] = y.astype(BF16)
        kd = _stack_heads(k * kdec_ref[gi], masks).astype(BF16)
        st_ref[gi] = st * sdec_ref[gi] + _tn(jnp.concatenate(vs, axis=0), kd)


def _post_stage(final, o_branches, gt_ref, x_ref, wbr_refs, wout_ref, nffn_ref, wg_ref, wu_ref, wd_ref,
                nfin_ref, out_ref):
    d = x_ref.shape[1]
    merged = None
    for i, o_branch in enumerate(o_branches):
        gate = _sigmoid(gt_ref[:, i * d:(i + 1) * d].astype(F32))
        term = gate * _mm(o_branch, wbr_refs[i][...])
        merged = term if merged is None else merged + term
    h = x_ref[...] + _mm(merged.astype(BF16), wout_ref[...])
    hn = _rmsnorm(h, nffn_ref[...]).astype(BF16)
    g = _mm(hn, wg_ref[...])
    u = _mm(hn, wu_ref[...])
    y = h + _mm((g * _sigmoid(g) * u).astype(BF16), wd_ref[...])
    if final:
        y = _rmsnorm(y, nfin_ref[...])
    out_ref[...] = y


def _mixpost_kernel(final, per_seq,
                    qa_ref, ka_ref, va_ref, lfa_ref, oga_ref, qb_ref, kb_ref, vb_ref, ogb_ref,
                    qc_ref, kc_ref, vc_ref, lgc_ref, ogc_ref, gt_ref, x_ref,
                    gna_ref, gnc_ref, dmask_ref, qdec_ref, kdec_ref, sdec_ref,
                    wbra_ref, wbrb_ref, wbrc_ref, wout_ref, nffn_ref, wg_ref, wu_ref, wd_ref, nfin_ref,
                    out_ref,
                    sta_ref, stb_ref, stc_ref, b_sc, kf_sc, o_sc):
    s = pl.program_id(0)

    @pl.when(s % per_seq == 0)
    def _():
        sta_ref[...] = jnp.zeros_like(sta_ref)
        stb_ref[...] = jnp.zeros_like(stb_ref)
        stc_ref[...] = jnp.zeros_like(stc_ref)

    @pl.when(s == 0)
    def _():
        o_sc[...] = jnp.zeros_like(o_sc)

    width = N_HEADS * HEAD_DV
    o_prev = [o_sc[:, i * width:(i + 1) * width] for i in range(3)]
    _gla_stage(1, qa_ref, ka_ref, va_ref, lfa_ref, oga_ref, gna_ref, o_sc, 0, sta_ref, b_sc, kf_sc)
    _ret_stage(qb_ref, kb_ref, vb_ref, ogb_ref, dmask_ref, qdec_ref, kdec_ref, sdec_ref, o_sc, width,
               stb_ref, kf_sc)
    _gla_stage(2, qc_ref, kc_ref, vc_ref, lgc_ref, ogc_ref, gnc_ref, o_sc, 2 * width, stc_ref, b_sc,
               kf_sc)
    _post_stage(final, o_prev, gt_ref, x_ref, (wbra_ref, wbrb_ref, wbrc_ref), wout_ref, nffn_ref, wg_ref,
                wu_ref, wd_ref, nfin_ref, out_ref)


def _mixpost(final, mix_rows, gt, xf, consts, seq_len, tb):
    tokens, d = xf.shape
    n_blocks = tokens // tb
    per_seq = seq_len // tb
    width = N_HEADS * HEAD_DV
    cur = lambda a: pl.BlockSpec((tb, a.shape[1]), lambda s: (jnp.minimum(s, n_blocks - 1), 0))
    prev = lambda a: pl.BlockSpec((tb, a.shape[1]), lambda s: (jnp.maximum(s - 1, 0), 0))
    const = lambda a: pl.BlockSpec(a.shape, lambda s: (0,) * a.ndim, pipeline_mode=pl.Buffered(1))
    state = lambda n: pltpu.VMEM((n, HEAD_DV, LANES), F32)
    rows_f32 = lambda w: pltpu.VMEM((tb, w), F32)
    return pl.pallas_call(
        functools.partial(_mixpost_kernel, final, per_seq),
        grid=(n_blocks + 1,),
        in_specs=[cur(a) for a in mix_rows] + [prev(gt), prev(xf)] + [const(a) for a in consts],
        out_specs=pl.BlockSpec((tb, d), lambda s: (jnp.maximum(s - 1, 0), 0)),
        out_shape=jax.ShapeDtypeStruct((tokens, d), F32),
        scratch_shapes=[state(4), state(2), state(2), rows_f32(512), rows_f32(512),
                        pltpu.VMEM((tb, 3 * width), BF16)],
        compiler_params=pltpu.CompilerParams(
            dimension_semantics=("arbitrary",), vmem_limit_bytes=V7X_SCOPED_VMEM_BYTES),
        name="mixpost",
    )(*mix_rows, gt, xf, *consts)


def _rotary_tables(seq_len):
    inv_freq = 1.0 / (ROPE_BASE ** jnp.linspace(0.0, 1.0, 32, dtype=F32))
    ang = jnp.arange(seq_len, dtype=F32)[:, None] * inv_freq[None, :]
    sin, cos = jnp.sin(ang), jnp.cos(ang)
    cos_t = jnp.tile(jnp.repeat(cos, 2, axis=1), (1, N_HEADS))
    sin_t = jnp.tile(jnp.stack([-sin, sin], axis=-1).reshape(seq_len, 64), (1, N_HEADS))
    return cos_t, sin_t


def _retention_tables(cb):
    log_gamma = jnp.log(1.0 - 2.0 ** (-5.0 - jnp.arange(N_HEADS, dtype=F32)))
    pos = jnp.arange(cb, dtype=F32)
    diff = pos[:, None] - pos[None, :]
    dmask = jnp.where(diff >= 0, jnp.exp(diff[None] * log_gamma[:, None, None]), 0.0)
    lane_gamma = jnp.repeat(log_gamma, 64).reshape(N_HEADS // 2, 1, LANES)
    qdec = jnp.exp((pos[None, :, None] + 1.0) * lane_gamma)
    kdec = jnp.exp((cb - 1.0 - pos[None, :, None]) * lane_gamma)
    sdec = jnp.exp(float(cb) * lane_gamma)
    return dmask.reshape(N_HEADS // 2, 2 * cb, cb), qdec, kdec, sdec


def kernel(x, norm_mix, w_in, lb_logits, w_gk_up, b_gk, gn_a, gn_c, w_br_a, w_br_b, w_br_c, w_out,
           norm_ffn, w_ffn_gate, w_ffn_up, w_ffn_down, norm_final):
    batch, seq_len, d = x.shape
    tokens = batch * seq_len
    tm = min(512, seq_len)
    tb = min(256, seq_len)

    w_main = w_in[:, :, :_C_R].astype(BF16)
    w_code = jnp.pad(w_in[:, :, _C_R:_GATES], ((0, 0), (0, 0), (0, LANES - GK_RANK))).astype(BF16)
    w_gates = w_in[:, :, _GATES:].astype(BF16)
    wup = jnp.pad(w_gk_up, ((0, 0), (0, LANES - GK_RANK), (0, 0))).astype(BF16)
    wbr = [w.astype(BF16) for w in (w_br_a, w_br_b, w_br_c)]
    wout = w_out.astype(BF16)
    wg, wu, wd = w_ffn_gate.astype(BF16), w_ffn_up.astype(BF16), w_ffn_down.astype(BF16)
    cos_t, sin_t = _rotary_tables(seq_len)
    dmask, qdec, kdec, sdec = _retention_tables(tb)
    nfin = norm_final.reshape(1, d)

    xf = x.reshape(tokens, d)
    for layer in range(DEPTH):
        (qa, ka, va, lfa, oga, qb, kb, vb, ogb, qc, kc, vc, lgc, ogc, gt) = _inproj(
            layer, xf, norm_mix[layer].reshape(1, d), w_main[layer], w_code[layer], w_gates[layer], lb_logits,
            cos_t, sin_t,
            wup[layer], b_gk[layer].reshape(1, -1), seq_len, tm)
        consts = [gn_a[layer].reshape(1, -1), gn_c[layer].reshape(1, -1), dmask, qdec, kdec, sdec,
                  wbr[0][layer], wbr[1][layer], wbr[2][layer], wout[layer], norm_ffn[layer].reshape(1, d),
                  wg[layer], wu[layer], wd[layer], nfin]
        xf = _mixpost(layer == DEPTH - 1, [qa, ka, va, lfa, oga, qb, kb, vb, ogb, qc, kc, vc, lgc, ogc],
                      gt, xf, consts, seq_len, tb)
    return xf.reshape(batch, seq_len, d)
```

```python
import functools

import jax
import jax.numpy as jnp
from jax import lax
from jax.experimental import pallas as pl
from jax.experimental.pallas import tpu as pltpu

F32 = jnp.float32
BF16 = jnp.bfloat16

DEPTH = 4
CHUNK = 64
SUB = 8
NORM_EPS = 1e-6
N_HEADS = 4
HEAD_DV = 128
ROPE_BASE = 10000.0
GK_RANK = 16
GK_NORMALIZER = 16.0
LOG2E = 1.4426950408889634
LANES = 128
V7X_SCOPED_VMEM_BYTES = 60000 * 1024

_A_Q, _A_F, _A_I, _A_G = 0, 512, 1024, 1536
_B_Q, _B_K, _B_V, _B_G = 2048, 2304, 2560, 3072
_C_Q, _C_K, _C_V, _C_G = 3584, 3840, 4096, 4608
_C_R = 5120
_GATES = _C_R + GK_RANK


def _nt(a, b):
    return lax.dot_general(a, b, (((1,), (1,)), ((), ())), preferred_element_type=F32)


def _tn(a, b):
    return lax.dot_general(a, b, (((0,), (0,)), ((), ())), preferred_element_type=F32)


def _mm(a, b):
    return jnp.dot(a, b, preferred_element_type=F32)


def _sigmoid(z):
    return 1.0 / (1.0 + jnp.exp(-z))


def _log_sigmoid(z):
    return jnp.minimum(z, 0.0) - jnp.log1p(jnp.exp(-jnp.abs(z)))


def _rmsnorm(x, w):
    ms = jnp.mean(x * x, axis=-1, keepdims=True)
    return x * lax.rsqrt(ms + NORM_EPS) * w


def _swap_lane_pairs(t):
    n = t.shape[1]
    lane = lax.broadcasted_iota(jnp.int32, t.shape, 1)
    return jnp.where(lane % 2 == 0, pltpu.roll(t, n - 1, 1), pltpu.roll(t, 1, 1))


def _inproj_kernel(layer, x_ref, nw_ref, w_ref, wr_ref, wgt_ref, lbl_ref, cos_ref, sin_ref, wup_ref, bgk_ref,
                   qa_ref, ka_ref, va_ref, lfa_ref, oga_ref,
                   qb_ref, kb_ref, vb_ref, ogb_ref,
                   qc_ref, kc_ref, vc_ref, lgc_ref, ogc_ref, gt_ref):
    xn = _rmsnorm(x_ref[...], nw_ref[...]).astype(BF16)

    def proj(lo, hi):
        return _mm(xn, w_ref[:, lo:hi])

    lbl = lbl_ref[...]
    e = jnp.exp(lbl - jnp.max(lbl, axis=0, keepdims=True))
    p = e / jnp.sum(e, axis=0, keepdims=True)
    c = p[0:1]
    for r in range(1, layer + 1):
        c = c + p[r:r + 1]
    lb = c - p[0:1]

    aq = proj(_A_Q, _A_F)
    qa_ref[...] = (aq * _sigmoid(aq) * (HEAD_DV ** -0.5)).astype(BF16)
    z = proj(_A_F, _A_I)
    t = jnp.exp(-jnp.abs(z))
    pos = z >= 0.0
    log_num = jnp.log(jnp.where(pos, 1.0, lb) + jnp.where(pos, lb, 1.0) * t)
    log_num = jnp.where(jnp.logical_or(pos, lb > 0.0), log_num, z)
    lfa_ref[...] = log_num - jnp.log1p(t)
    ka_ref[...] = ((1.0 - lb) * jnp.where(pos, t, 1.0) / (1.0 + t)).astype(BF16)
    va_ref[...] = proj(_A_I, _A_G).astype(BF16)
    ga = proj(_A_G, _B_Q)
    oga_ref[...] = (ga * _sigmoid(ga)).astype(BF16)

    cos = cos_ref[...]
    sin = sin_ref[...]
    qb = proj(_B_Q, _B_K)
    qb_ref[...] = (qb * cos + _swap_lane_pairs(qb) * sin).astype(BF16)
    kb = proj(_B_K, _B_V)
    kb_ref[...] = ((kb * cos + _swap_lane_pairs(kb) * sin) * (64 ** -0.5)).astype(BF16)
    vb_ref[...] = proj(_B_V, _B_G).astype(BF16)
    gb = proj(_B_G, _C_Q)
    ogb_ref[...] = (gb * _sigmoid(gb)).astype(BF16)

    qc_ref[...] = (proj(_C_Q, _C_K) * (64 ** -0.5)).astype(BF16)
    kc_ref[...] = proj(_C_K, _C_V).astype(BF16)
    vc_ref[...] = proj(_C_V, _C_G).astype(BF16)
    gc = proj(_C_G, _C_R)
    ogc_ref[...] = (gc * _sigmoid(gc)).astype(BF16)
    code = _mm(xn, wr_ref[...]).astype(BF16)
    gk = _mm(code, wup_ref[...]) + bgk_ref[...]
    lgc_ref[...] = _log_sigmoid(gk) * (1.0 / GK_NORMALIZER)

    gt_ref[...] = _mm(xn, wgt_ref[...]).astype(BF16)


def _inproj(layer, xf, nw, w, wr, wgt, lbl, cos_t, sin_t, wup, bgk, seq_len, tm):
    tokens, d = xf.shape
    nt = tokens // tm
    per_seq = seq_len // tm

    def rows(width, dtype):
        return jax.ShapeDtypeStruct((tokens, width), dtype), pl.BlockSpec((tm, width), lambda i: (i, 0))

    outs = [rows(512, BF16), rows(512, BF16), rows(512, BF16), rows(512, F32), rows(512, BF16),
            rows(256, BF16), rows(256, BF16), rows(512, BF16), rows(512, BF16),
            rows(256, BF16), rows(256, BF16), rows(512, BF16), rows(256, F32), rows(512, BF16),
            rows(3072, BF16)]
    const = lambda shape: pl.BlockSpec(shape, lambda i: (0,) * len(shape), pipeline_mode=pl.Buffered(1))
    in_specs = [
        pl.BlockSpec((tm, d), lambda i: (i, 0)),
        const((1, d)),
        const(w.shape),
        const(wr.shape),
        const(wgt.shape),
        const(lbl.shape),
        pl.BlockSpec((tm, 256), lambda i: (i % per_seq, 0)),
        pl.BlockSpec((tm, 256), lambda i: (i % per_seq, 0)),
        const(wup.shape),
        const(bgk.shape),
    ]
    return pl.pallas_call(
        functools.partial(_inproj_kernel, layer),
        grid=(nt,),
        in_specs=in_specs,
        out_specs=[o[1] for o in outs],
        out_shape=[o[0] for o in outs],
        compiler_params=pltpu.CompilerParams(
            dimension_semantics=("parallel",), vmem_limit_bytes=V7X_SCOPED_VMEM_BYTES),
        name="inproj",
    )(xf, nw, w, wr, wgt, lbl, cos_t, sin_t, wup, bgk)


def _head_masks(nh):
    lane = lax.broadcasted_iota(jnp.int32, (1, LANES), 1)
    return [(lane // (LANES // nh)) == m for m in range(nh)]


def _stack_heads(x, masks):
    if len(masks) == 1:
        return x
    return jnp.concatenate([jnp.where(m, x, 0.0) for m in masks], axis=0)


def _cumsum_rows_f32(tri, g):
    hi = g.astype(BF16)
    r1 = g - hi.astype(F32)
    mid = r1.astype(BF16)
    lo = (r1 - mid.astype(F32)).astype(BF16)
    return _mm(tri, hi) + _mm(tri, mid) + _mm(tri, lo)


def _gla_chunk_scores(qs, bs, kf_sc, b_sc, r0, masks):
    ng, nh = len(qs), len(masks)
    lane_c = lax.broadcasted_iota(jnp.int32, (SUB, CHUNK), 1)
    row_c = lax.broadcasted_iota(jnp.int32, (SUB, CHUNK), 0)
    blocks = [[[] for _ in range(nh)] for _ in range(ng)]
    kk_rows = [[] for _ in range(ng)]
    for blk in range(CHUNK // SUB):
        lo = blk * SUB
        causal = (lane_c - lo) <= row_c
        dg = [[jnp.zeros((SUB, CHUNK), F32) for _ in range(nh)] for _ in range(ng)]
        for jj in range(SUB):
            j = lo + jj
            hit = lane_c == j
            for gi in range(ng):
                lanes = slice(gi * LANES, (gi + 1) * LANES)
                kj = kf_sc[r0 + j:r0 + j + 1, lanes]
                bj = b_sc[r0 + j:r0 + j + 1, lanes]
                pr = qs[gi][lo:lo + SUB] * kj * jnp.exp2(bs[gi][lo:lo + SUB] - bj)
                for m in range(nh):
                    pm = pr if nh == 1 else jnp.where(masks[m], pr, 0.0)
                    dg[gi][m] = jnp.where(hit, jnp.sum(pm, axis=-1, keepdims=True), dg[gi][m])
        for gi in range(ng):
            lanes = slice(gi * LANES, (gi + 1) * LANES)
            blk_dg = [jnp.where(causal, d, 0.0) for d in dg[gi]]
            if blk > 0:
                ref_row = b_sc[r0 + lo - 1:r0 + lo, lanes]
                qp = qs[gi][lo:lo + SUB] * jnp.exp2(bs[gi][lo:lo + SUB] - ref_row)
                if blk > 1:
                    step = jnp.exp2(ref_row - b_sc[r0 + lo - SUB - 1:r0 + lo - SUB, lanes])
                    kk_rows[gi] = [kr * step for kr in kk_rows[gi]]
                kk_rows[gi].append(kf_sc[r0 + lo - SUB:r0 + lo, lanes]
                                   * jnp.exp2(ref_row - bs[gi][lo - SUB:lo]))
                kk = jnp.concatenate(kk_rows[gi] + [jnp.zeros((CHUNK - lo, LANES), F32)], axis=0).astype(BF16)
                lhs = _stack_heads(qp, masks)
                if lhs.shape[0] < 16:
                    lhs = jnp.concatenate([lhs, jnp.zeros((16 - lhs.shape[0], LANES), F32)], axis=0)
                off = _nt(lhs.astype(BF16), kk)
                blk_dg = [off[m * SUB:(m + 1) * SUB] + blk_dg[m] for m in range(nh)]
            for m in range(nh):
                blocks[gi][m].append(blk_dg[m])
    return [[jnp.concatenate(bl, axis=0) for bl in grp] for grp in blocks]


def _gla_stage(nh, q_ref, k_ref, v_ref, g_ref, og_ref, gn_ref, o_ref, o_base, st_ref, b_sc, kf_sc):
    tb = q_ref.shape[0]
    n_groups = q_ref.shape[1] // LANES
    masks = _head_masks(nh)
    gn = gn_ref[...]
    ri = lax.broadcasted_iota(jnp.int32, (tb, tb), 0)
    ci = lax.broadcasted_iota(jnp.int32, (tb, tb), 1)
    tri = jnp.where(ri >= ci, jnp.where(ri // CHUNK == ci // CHUNK, 1.0, 0.0), 0.0).astype(BF16)
    w = q_ref.shape[1]
    b_sc[:, 0:w] = _cumsum_rows_f32(tri, g_ref[...]) * LOG2E
    kf_sc[:, 0:w] = k_ref[...].astype(F32)

    for c in range(tb // CHUNK):
        r0 = c * CHUNK
        rows = slice(r0, r0 + CHUNK)
        qs = [q_ref[rows, gi * LANES:(gi + 1) * LANES].astype(F32) for gi in range(n_groups)]
        bs = [b_sc[rows, gi * LANES:(gi + 1) * LANES] for gi in range(n_groups)]
        scores = _gla_chunk_scores(qs, bs, kf_sc, b_sc, r0, masks)
        for gi in range(n_groups):
            lanes = slice(gi * LANES, (gi + 1) * LANES)
            b = bs[gi]
            b_last = b_sc[r0 + CHUNK - 1:r0 + CHUNK, lanes]
            st = st_ref[gi]
            inter = _nt(_stack_heads(qs[gi] * jnp.exp2(b), masks).astype(BF16), st.astype(BF16))
            vs = []
            for m in range(nh):
                vl = slice((gi * nh + m) * HEAD_DV, (gi * nh + m + 1) * HEAD_DV)
                v = v_ref[rows, vl]
                vs.append(v)
                o = _mm(scores[gi][m].astype(BF16), v) + inter[m * CHUNK:(m + 1) * CHUNK]
                y = _rmsnorm(o, gn) * og_ref[rows, vl].astype(F32)
                o_ref[rows, o_base + vl.start:o_base + vl.stop] = y.astype(BF16)
            kd = _stack_heads(kf_sc[rows, lanes] * jnp.exp2(b_last - b), masks).astype(BF16)
            vcat = vs[0] if nh == 1 else jnp.concatenate(vs, axis=0)
            st_ref[gi] = st * jnp.exp2(b_last) + _tn(vcat, kd)


def _ret_stage(q_ref, k_ref, v_ref, og_ref, dmask_ref, qdec_ref, kdec_ref, sdec_ref, o_ref, o_base, st_ref,
               kf_sc):
    nh = 2
    cb = q_ref.shape[0]
    masks = _head_masks(nh)
    kf_sc[:, 0:q_ref.shape[1]] = k_ref[...].astype(F32)
    for gi in range(q_ref.shape[1] // LANES):
        lanes = slice(gi * LANES, (gi + 1) * LANES)
        q = q_ref[:, lanes].astype(F32)
        k = kf_sc[:, lanes]
        sc = _nt(_stack_heads(q, masks).astype(BF16), k_ref[:, lanes]) * dmask_ref[gi]
        st = st_ref[gi]
        inter = _nt(_stack_heads(q * qdec_ref[gi], masks).astype(BF16), st.astype(BF16))
        vs = []
        for m in range(nh):
            vl = slice((gi * nh + m) * HEAD_DV, (gi * nh + m + 1) * HEAD_DV)
            v = v_ref[:, vl]
            vs.append(v)
            o = _mm(sc[m * cb:(m + 1) * cb].astype(BF16), v) + inter[m * cb:(m + 1) * cb]
            mu = jnp.mean(o, axis=-1, keepdims=True)
            var = jnp.mean(jnp.square(o - mu), axis=-1, keepdims=True)
            y = (o - mu) * lax.rsqrt(var + NORM_EPS) * og_ref[:, vl].astype(F32)
            o_ref[:, o_base + vl.start:o_base + vl.stop] = y.astype(BF16)
        kd = _stack_heads(k * kdec_ref[gi], masks).astype(BF16)
        st_ref[gi] = st * sdec_ref[gi] + _tn(jnp.concatenate(vs, axis=0), kd)


def _post_stage(final, o_branches, gt_ref, x_ref, wbr_refs, wout_ref, nffn_ref, wg_ref, wu_ref, wd_ref,
                nfin_ref, out_ref):
    d = x_ref.shape[1]
    merged = None
    for i, o_branch in enumerate(o_branches):
        gate = _sigmoid(gt_ref[:, i * d:(i + 1) * d].astype(F32))
        term = gate * _mm(o_branch, wbr_refs[i][...])
        merged = term if merged is None else merged + term
    h = x_ref[...] + _mm(merged.astype(BF16), wout_ref[...])
    hn = _rmsnorm(h, nffn_ref[...]).astype(BF16)
    g = _mm(hn, wg_ref[...])
    u = _mm(hn, wu_ref[...])
    y = h + _mm((g * _sigmoid(g) * u).astype(BF16), wd_ref[...])
    if final:
        y = _rmsnorm(y, nfin_ref[...])
    out_ref[...] = y


def _mixpost_kernel(final, per_seq,
                    qa_ref, ka_ref, va_ref, lfa_ref, oga_ref, qb_ref, kb_ref, vb_ref, ogb_ref,
                    qc_ref, kc_ref, vc_ref, lgc_ref, ogc_ref, gt_ref, x_ref,
                    gna_ref, gnc_ref, dmask_ref, qdec_ref, kdec_ref, sdec_ref,
                    wbra_ref, wbrb_ref, wbrc_ref, wout_ref, nffn_ref, wg_ref, wu_ref, wd_ref, nfin_ref,
                    out_ref,
                    sta_ref, stb_ref, stc_ref, b_sc, kf_sc, o_sc):
    s = pl.program_id(0)

    @pl.when(s % per_seq == 0)
    def _():
        sta_ref[...] = jnp.zeros_like(sta_ref)
        stb_ref[...] = jnp.zeros_like(stb_ref)
        stc_ref[...] = jnp.zeros_like(stc_ref)

    @pl.when(s == 0)
    def _():
        o_sc[...] = jnp.zeros_like(o_sc)

    width = N_HEADS * HEAD_DV
    o_prev = [o_sc[:, i * width:(i + 1) * width] for i in range(3)]
    _gla_stage(1, qa_ref, ka_ref, va_ref, lfa_ref, oga_ref, gna_ref, o_sc, 0, sta_ref, b_sc, kf_sc)
    _ret_stage(qb_ref, kb_ref, vb_ref, ogb_ref, dmask_ref, qdec_ref, kdec_ref, sdec_ref, o_sc, width,
               stb_ref, kf_sc)
    _gla_stage(2, qc_ref, kc_ref, vc_ref, lgc_ref, ogc_ref, gnc_ref, o_sc, 2 * width, stc_ref, b_sc,
               kf_sc)
    _post_stage(final, o_prev, gt_ref, x_ref, (wbra_ref, wbrb_ref, wbrc_ref), wout_ref, nffn_ref, wg_ref,
                wu_ref, wd_ref, nfin_ref, out_ref)


def _mixpost(final, mix_rows, gt, xf, consts, seq_len, tb):
    tokens, d = xf.shape
    n_blocks = tokens // tb
    per_seq = seq_len // tb
    width = N_HEADS * HEAD_DV
    cur = lambda a: pl.BlockSpec((tb, a.shape[1]), lambda s: (jnp.minimum(s, n_blocks - 1), 0))
    prev = lambda a: pl.BlockSpec((tb, a.shape[1]), lambda s: (jnp.maximum(s - 1, 0), 0))
    const = lambda a: pl.BlockSpec(a.shape, lambda s: (0,) * a.ndim, pipeline_mode=pl.Buffered(1))
    state = lambda n: pltpu.VMEM((n, HEAD_DV, LANES), F32)
    rows_f32 = lambda w: pltpu.VMEM((tb, w), F32)
    return pl.pallas_call(
        functools.partial(_mixpost_kernel, final, per_seq),
        grid=(n_blocks + 1,),
        in_specs=[cur(a) for a in mix_rows] + [prev(gt), prev(xf)] + [const(a) for a in consts],
        out_specs=pl.BlockSpec((tb, d), lambda s: (jnp.maximum(s - 1, 0), 0)),
        out_shape=jax.ShapeDtypeStruct((tokens, d), F32),
        scratch_shapes=[state(4), state(2), state(2), rows_f32(512), rows_f32(512),
                        pltpu.VMEM((tb, 3 * width), BF16)],
        compiler_params=pltpu.CompilerParams(
            dimension_semantics=("arbitrary",), vmem_limit_bytes=V7X_SCOPED_VMEM_BYTES),
        name="mixpost",
    )(*mix_rows, gt, xf, *consts)


def _rotary_tables(seq_len):
    inv_freq = 1.0 / (ROPE_BASE ** jnp.linspace(0.0, 1.0, 32, dtype=F32))
    ang = jnp.arange(seq_len, dtype=F32)[:, None] * inv_freq[None, :]
    sin, cos = jnp.sin(ang), jnp.cos(ang)
    cos_t = jnp.tile(jnp.repeat(cos, 2, axis=1), (1, N_HEADS))
    sin_t = jnp.tile(jnp.stack([-sin, sin], axis=-1).reshape(seq_len, 64), (1, N_HEADS))
    return cos_t, sin_t


def _retention_tables(cb):
    log_gamma = jnp.log(1.0 - 2.0 ** (-5.0 - jnp.arange(N_HEADS, dtype=F32)))
    pos = jnp.arange(cb, dtype=F32)
    diff = pos[:, None] - pos[None, :]
    dmask = jnp.where(diff >= 0, jnp.exp(diff[None] * log_gamma[:, None, None]), 0.0)
    lane_gamma = jnp.repeat(log_gamma, 64).reshape(N_HEADS // 2, 1, LANES)
    qdec = jnp.exp((pos[None, :, None] + 1.0) * lane_gamma)
    kdec = jnp.exp((cb - 1.0 - pos[None, :, None]) * lane_gamma)
    sdec = jnp.exp(float(cb) * lane_gamma)
    return dmask.reshape(N_HEADS // 2, 2 * cb, cb), qdec, kdec, sdec


def kernel(x, norm_mix, w_in, lb_logits, w_gk_up, b_gk, gn_a, gn_c, w_br_a, w_br_b, w_br_c, w_out,
           norm_ffn, w_ffn_gate, w_ffn_up, w_ffn_down, norm_final):
    batch, seq_len, d = x.shape
    tokens = batch * seq_len
    tm = min(512, seq_len)
    tb = min(256, seq_len)

    w_main = w_in[:, :, :_C_R].astype(BF16)
    w_code = jnp.pad(w_in[:, :, _C_R:_GATES], ((0, 0), (0, 0), (0, LANES - GK_RANK))).astype(BF16)
    w_gates = w_in[:, :, _GATES:].astype(BF16)
    wup = jnp.pad(w_gk_up, ((0, 0), (0, LANES - GK_RANK), (0, 0))).astype(BF16)
    wbr = [w.astype(BF16) for w in (w_br_a, w_br_b, w_br_c)]
    wout = w_out.astype(BF16)
    wg, wu, wd = w_ffn_gate.astype(BF16), w_ffn_up.astype(BF16), w_ffn_down.astype(BF16)
    cos_t, sin_t = _rotary_tables(seq_len)
    dmask, qdec, kdec, sdec = _retention_tables(tb)
    nfin = norm_final.reshape(1, d)

    xf = x.reshape(tokens, d)
    for layer in range(DEPTH):
        (qa, ka, va, lfa, oga, qb, kb, vb, ogb, qc, kc, vc, lgc, ogc, gt) = _inproj(
            layer, xf, norm_mix[layer].reshape(1, d), w_main[layer], w_code[layer], w_gates[layer], lb_logits,
            cos_t, sin_t,
            wup[layer], b_gk[layer].reshape(1, -1), seq_len, tm)
        consts = [gn_a[layer].reshape(1, -1), gn_c[layer].reshape(1, -1), dmask, qdec, kdec, sdec,
                  wbr[0][layer], wbr[1][layer], wbr[2][layer], wout[layer], norm_ffn[layer].reshape(1, d),
                  wg[layer], wu[layer], wd[layer], nfin]
        xf = _mixpost(layer == DEPTH - 1, [qa, ka, va, lfa, oga, qb, kb, vb, ogb, qc, kc, vc, lgc, ogc],
                      gt, xf, consts, seq_len, tb)
    return xf.reshape(batch, seq_len, d)
```

```python
import functools

import jax
import jax.numpy as jnp
from jax import lax
from jax.experimental import pallas as pl
from jax.experimental.pallas import tpu as pltpu

F32 = jnp.float32
BF16 = jnp.bfloat16

DEPTH = 4
CHUNK = 64
SUB = 8
NORM_EPS = 1e-6
N_HEADS = 4
HEAD_DV = 128
ROPE_BASE = 10000.0
GK_RANK = 16
GK_NORMALIZER = 16.0
LOG2E = 1.4426950408889634
LANES = 128
V7X_SCOPED_VMEM_BYTES = 60000 * 1024

_A_Q, _A_F, _A_I, _A_G = 0, 512, 1024, 1536
_B_Q, _B_K, _B_V, _B_G = 2048, 2304, 2560, 3072
_C_Q, _C_K, _C_V, _C_G = 3584, 3840, 4096, 4608
_C_R = 5120
_GATES = _C_R + GK_RANK


def _nt(a, b):
    return lax.dot_general(a, b, (((1,), (1,)), ((), ())), preferred_element_type=F32)


def _tn(a, b):
    return lax.dot_general(a, b, (((0,), (0,)), ((), ())), preferred_element_type=F32)


def _mm(a, b):
    return jnp.dot(a, b, preferred_element_type=F32)


def _sigmoid(z):
    return 1.0 / (1.0 + jnp.exp(-z))


def _log_sigmoid(z):
    return jnp.minimum(z, 0.0) - jnp.log1p(jnp.exp(-jnp.abs(z)))


def _rmsnorm(x, w):
    ms = jnp.mean(x * x, axis=-1, keepdims=True)
    return x * lax.rsqrt(ms + NORM_EPS) * w


def _swap_lane_pairs(t):
    n = t.shape[1]
    lane = lax.broadcasted_iota(jnp.int32, t.shape, 1)
    return jnp.where(lane % 2 == 0, pltpu.roll(t, n - 1, 1), pltpu.roll(t, 1, 1))


def _inproj_kernel(layer, x_ref, nw_ref, w_ref, wr_ref, wgt_ref, lbl_ref, cos_ref, sin_ref, wup_ref, bgk_ref,
                   qa_ref, ka_ref, va_ref, lfa_ref, oga_ref,
                   qb_ref, kb_ref, vb_ref, ogb_ref,
                   qc_ref, kc_ref, vc_ref, lgc_ref, ogc_ref, gt_ref):
    xn = _rmsnorm(x_ref[...], nw_ref[...]).astype(BF16)

    def proj(lo, hi):
        return _mm(xn, w_ref[:, lo:hi])

    lbl = lbl_ref[...]
    e = jnp.exp(lbl - jnp.max(lbl, axis=0, keepdims=True))
    p = e / jnp.sum(e, axis=0, keepdims=True)
    c = p[0:1]
    for r in range(1, layer + 1):
        c = c + p[r:r + 1]
    lb = c - p[0:1]

    aq = proj(_A_Q, _A_F)
    qa_ref[...] = (aq * _sigmoid(aq) * (HEAD_DV ** -0.5)).astype(BF16)
    z = proj(_A_F, _A_I)
    t = jnp.exp(-jnp.abs(z))
    pos = z >= 0.0
    log_num = jnp.log(jnp.where(pos, 1.0, lb) + jnp.where(pos, lb, 1.0) * t)
    log_num = jnp.where(jnp.logical_or(pos, lb > 0.0), log_num, z)
    lfa_ref[...] = log_num - jnp.log1p(t)
    ka_ref[...] = ((1.0 - lb) * jnp.where(pos, t, 1.0) / (1.0 + t)).astype(BF16)
    va_ref[...] = proj(_A_I, _A_G).astype(BF16)
    ga = proj(_A_G, _B_Q)
    oga_ref[...] = (ga * _sigmoid(ga)).astype(BF16)

    cos = cos_ref[...]
    sin = sin_ref[...]
    qb = proj(_B_Q, _B_K)
    qb_ref[...] = (qb * cos + _swap_lane_pairs(qb) * sin).astype(BF16)
    kb = proj(_B_K, _B_V)
    kb_ref[...] = ((kb * cos + _swap_lane_pairs(kb) * sin) * (64 ** -0.5)).astype(BF16)
    vb_ref[...] = proj(_B_V, _B_G).astype(BF16)
    gb = proj(_B_G, _C_Q)
    ogb_ref[...] = (gb * _sigmoid(gb)).astype(BF16)

    qc_ref[...] = (proj(_C_Q, _C_K) * (64 ** -0.5)).astype(BF16)
    kc_ref[...] = proj(_C_K, _C_V).astype(BF16)
    vc_ref[...] = proj(_C_V, _C_G).astype(BF16)
    gc = proj(_C_G, _C_R)
    ogc_ref[...] = (gc * _sigmoid(gc)).astype(BF16)
    code = _mm(xn, wr_ref[...]).astype(BF16)
    gk = _mm(code, wup_ref[...]) + bgk_ref[...]
    lgc_ref[...] = _log_sigmoid(gk) * (1.0 / GK_NORMALIZER)

    gt_ref[...] = _mm(xn, wgt_ref[...]).astype(BF16)


def _inproj(layer, xf, nw, w, wr, wgt, lbl, cos_t, sin_t, wup, bgk, seq_len, tm):
    tokens, d = xf.shape
    nt = tokens // tm
    per_seq = seq_len // tm

    def rows(width, dtype):
        return jax.ShapeDtypeStruct((tokens, width), dtype), pl.BlockSpec((tm, width), lambda i: (i, 0))

    outs = [rows(512, BF16), rows(512, BF16), rows(512, BF16), rows(512, F32), rows(512, BF16),
            rows(256, BF16), rows(256, BF16), rows(512, BF16), rows(512, BF16),
            rows(256, BF16), rows(256, BF16), rows(512, BF16), rows(256, F32), rows(512, BF16),
            rows(3072, BF16)]
    const = lambda shape: pl.BlockSpec(shape, lambda i: (0,) * len(shape), pipeline_mode=pl.Buffered(1))
    in_specs = [
        pl.BlockSpec((tm, d), lambda i: (i, 0)),
        const((1, d)),
        const(w.shape),
        const(wr.shape),
        const(wgt.shape),
        const(lbl.shape),
        pl.BlockSpec((tm, 256), lambda i: (i % per_seq, 0)),
        pl.BlockSpec((tm, 256), lambda i: (i % per_seq, 0)),
        const(wup.shape),
        const(bgk.shape),
    ]
    return pl.pallas_call(
        functools.partial(_inproj_kernel, layer),
        grid=(nt,),
        in_specs=in_specs,
        out_specs=[o[1] for o in outs],
        out_shape=[o[0] for o in outs],
        compiler_params=pltpu.CompilerParams(
            dimension_semantics=("parallel",), vmem_limit_bytes=V7X_SCOPED_VMEM_BYTES),
        name="inproj",
    )(xf, nw, w, wr, wgt, lbl, cos_t, sin_t, wup, bgk)


def _head_masks(nh):
    lane = lax.broadcasted_iota(jnp.int32, (1, LANES), 1)
    return [(lane // (LANES // nh)) == m for m in range(nh)]


def _stack_heads(x, masks):
    if len(masks) == 1:
        return x
    return jnp.concatenate([jnp.where(m, x, 0.0) for m in masks], axis=0)


def _cumsum_rows_f32(tri, g):
    hi = g.astype(BF16)
    r1 = g - hi.astype(F32)
    mid = r1.astype(BF16)
    lo = (r1 - mid.astype(F32)).astype(BF16)
    return _mm(tri, hi) + _mm(tri, mid) + _mm(tri, lo)


def _gla_chunk_scores(qs, bs, kf_sc, b_sc, r0, masks):
    ng, nh = len(qs), len(masks)
    lane_c = lax.broadcasted_iota(jnp.int32, (SUB, CHUNK), 1)
    row_c = lax.broadcasted_iota(jnp.int32, (SUB, CHUNK), 0)
    blocks = [[[] for _ in range(nh)] for _ in range(ng)]
    kk_rows = [[] for _ in range(ng)]
    for blk in range(CHUNK // SUB):
        lo = blk * SUB
        causal = (lane_c - lo) <= row_c
        dg = [[jnp.zeros((SUB, CHUNK), F32) for _ in range(nh)] for _ in range(ng)]
        for jj in range(SUB):
            j = lo + jj
            hit = lane_c == j
            for gi in range(ng):
                lanes = slice(gi * LANES, (gi + 1) * LANES)
                kj = kf_sc[r0 + j:r0 + j + 1, lanes]
                bj = b_sc[r0 + j:r0 + j + 1, lanes]
                pr = qs[gi][lo:lo + SUB] * kj * jnp.exp2(bs[gi][lo:lo + SUB] - bj)
                for m in range(nh):
                    pm = pr if nh == 1 else jnp.where(masks[m], pr, 0.0)
                    dg[gi][m] = jnp.where(hit, jnp.sum(pm, axis=-1, keepdims=True), dg[gi][m])
        for gi in range(ng):
            lanes = slice(gi * LANES, (gi + 1) * LANES)
            blk_dg = [jnp.where(causal, d, 0.0) for d in dg[gi]]
            if blk > 0:
                ref_row = b_sc[r0 + lo - 1:r0 + lo, lanes]
                qp = qs[gi][lo:lo + SUB] * jnp.exp2(bs[gi][lo:lo + SUB] - ref_row)
                if blk > 1:
                    step = jnp.exp2(ref_row - b_sc[r0 + lo - SUB - 1:r0 + lo - SUB, lanes])
                    kk_rows[gi] = [kr * step for kr in kk_rows[gi]]
                kk_rows[gi].append(kf_sc[r0 + lo - SUB:r0 + lo, lanes]
                                   * jnp.exp2(ref_row - bs[gi][lo - SUB:lo]))
                kk = jnp.concatenate(kk_rows[gi] + [jnp.zeros((CHUNK - lo, LANES), F32)], axis=0).astype(BF16)
                lhs = _stack_heads(qp, masks)
                if lhs.shape[0] < 16:
                    lhs = jnp.concatenate([lhs, jnp.zeros((16 - lhs.shape[0], LANES), F32)], axis=0)
                off = _nt(lhs.astype(BF16), kk)
                blk_dg = [off[m * SUB:(m + 1) * SUB] + blk_dg[m] for m in range(nh)]
            for m in range(nh):
                blocks[gi][m].append(blk_dg[m])
    return [[jnp.concatenate(bl, axis=0) for bl in grp] for grp in blocks]


def _gla_stage(nh, q_ref, k_ref, v_ref, g_ref, og_ref, gn_ref, tri_ref, o_ref, o_base, st_ref, b_sc, kf_sc):
    tb = q_ref.shape[0]
    n_groups = q_ref.shape[1] // LANES
    masks = _head_masks(nh)
    gn = gn_ref[...]
    b_sc[...] = _cumsum_rows_f32(tri_ref[...], g_ref[...]) * LOG2E
    kf_sc[...] = k_ref[...].astype(F32)

    for c in range(tb // CHUNK):
        r0 = c * CHUNK
        rows = slice(r0, r0 + CHUNK)
        qs = [q_ref[rows, gi * LANES:(gi + 1) * LANES].astype(F32) for gi in range(n_groups)]
        bs = [b_sc[rows, gi * LANES:(gi + 1) * LANES] for gi in range(n_groups)]
        scores = _gla_chunk_scores(qs, bs, kf_sc, b_sc, r0, masks)
        for gi in range(n_groups):
            lanes = slice(gi * LANES, (gi + 1) * LANES)
            b = bs[gi]
            b_last = b_sc[r0 + CHUNK - 1:r0 + CHUNK, lanes]
            st = st_ref[gi]
            inter = _nt(_stack_heads(qs[gi] * jnp.exp2(b), masks).astype(BF16), st.astype(BF16))
            vs = []
            for m in range(nh):
                vl = slice((gi * nh + m) * HEAD_DV, (gi * nh + m + 1) * HEAD_DV)
                v = v_ref[rows, vl]
                vs.append(v)
                o = _mm(scores[gi][m].astype(BF16), v) + inter[m * CHUNK:(m + 1) * CHUNK]
                y = _rmsnorm(o, gn) * og_ref[rows, vl].astype(F32)
                o_ref[rows, o_base + vl.start:o_base + vl.stop] = y.astype(BF16)
            kd = _stack_heads(kf_sc[rows, lanes] * jnp.exp2(b_last - b), masks).astype(BF16)
            vcat = vs[0] if nh == 1 else jnp.concatenate(vs, axis=0)
            st_ref[gi] = st * jnp.exp2(b_last) + _tn(vcat, kd)


def _ret_stage(q_ref, k_ref, v_ref, og_ref, dmask_ref, qdec_ref, kdec_ref, sdec_ref, o_ref, o_base, st_ref):
    nh = 2
    cb = q_ref.shape[0]
    masks = _head_masks(nh)
    for gi in range(q_ref.shape[1] // LANES):
        lanes = slice(gi * LANES, (gi + 1) * LANES)
        q = q_ref[:, lanes].astype(F32)
        k = k_ref[:, lanes].astype(F32)
        sc = _nt(_stack_heads(q, masks).astype(BF16), k_ref[:, lanes]) * dmask_ref[gi]
        st = st_ref[gi]
        inter = _nt(_stack_heads(q * qdec_ref[gi], masks).astype(BF16), st.astype(BF16))
        vs = []
        for m in range(nh):
            vl = slice((gi * nh + m) * HEAD_DV, (gi * nh + m + 1) * HEAD_DV)
            v = v_ref[:, vl]
            vs.append(v)
            o = _mm(sc[m * cb:(m + 1) * cb].astype(BF16), v) + inter[m * cb:(m + 1) * cb]
            mu = jnp.mean(o, axis=-1, keepdims=True)
            var = jnp.mean(jnp.square(o - mu), axis=-1, keepdims=True)
            y = (o - mu) * lax.rsqrt(var + NORM_EPS) * og_ref[:, vl].astype(F32)
            o_ref[:, o_base + vl.start:o_base + vl.stop] = y.astype(BF16)
        kd = _stack_heads(k * kdec_ref[gi], masks).astype(BF16)
        st_ref[gi] = st * sdec_ref[gi] + _tn(jnp.concatenate(vs, axis=0), kd)


def _post_stage(final, o_branches, gt_ref, x_ref, wbr_refs, wout_ref, nffn_ref, wg_ref, wu_ref, wd_ref,
                nfin_ref, out_ref):
    d = x_ref.shape[1]
    merged = None
    for i, o_branch in enumerate(o_branches):
        gate = _sigmoid(gt_ref[:, i * d:(i + 1) * d].astype(F32))
        term = gate * _mm(o_branch, wbr_refs[i][...])
        merged = term if merged is None else merged + term
    h = x_ref[...] + _mm(merged.astype(BF16), wout_ref[...])
    hn = _rmsnorm(h, nffn_ref[...]).astype(BF16)
    g = _mm(hn, wg_ref[...])
    u = _mm(hn, wu_ref[...])
    y = h + _mm((g * _sigmoid(g) * u).astype(BF16), wd_ref[...])
    if final:
        y = _rmsnorm(y, nfin_ref[...])
    out_ref[...] = y


def _mixpost_kernel(final, per_seq,
                    qa_ref, ka_ref, va_ref, lfa_ref, oga_ref, qb_ref, kb_ref, vb_ref, ogb_ref,
                    qc_ref, kc_ref, vc_ref, lgc_ref, ogc_ref, gt_ref, x_ref,
                    gna_ref, gnc_ref, tri_ref, dmask_ref, qdec_ref, kdec_ref, sdec_ref,
                    wbra_ref, wbrb_ref, wbrc_ref, wout_ref, nffn_ref, wg_ref, wu_ref, wd_ref, nfin_ref,
                    out_ref,
                    sta_ref, stb_ref, stc_ref, ba_sc, kfa_sc, bc_sc, kfc_sc, o_sc):
    s = pl.program_id(0)

    @pl.when(s % per_seq == 0)
    def _():
        sta_ref[...] = jnp.zeros_like(sta_ref)
        stb_ref[...] = jnp.zeros_like(stb_ref)
        stc_ref[...] = jnp.zeros_like(stc_ref)

    @pl.when(s == 0)
    def _():
        o_sc[...] = jnp.zeros_like(o_sc)

    width = N_HEADS * HEAD_DV
    o_prev = [o_sc[:, i * width:(i + 1) * width] for i in range(3)]
    _gla_stage(1, qa_ref, ka_ref, va_ref, lfa_ref, oga_ref, gna_ref, tri_ref, o_sc, 0, sta_ref, ba_sc, kfa_sc)
    _ret_stage(qb_ref, kb_ref, vb_ref, ogb_ref, dmask_ref, qdec_ref, kdec_ref, sdec_ref, o_sc, width,
               stb_ref)
    _gla_stage(2, qc_ref, kc_ref, vc_ref, lgc_ref, ogc_ref, gnc_ref, tri_ref, o_sc, 2 * width, stc_ref,
               bc_sc, kfc_sc)
    _post_stage(final, o_prev, gt_ref, x_ref, (wbra_ref, wbrb_ref, wbrc_ref), wout_ref, nffn_ref, wg_ref,
                wu_ref, wd_ref, nfin_ref, out_ref)


def _mixpost(final, mix_rows, gt, xf, consts, seq_len, tb):
    tokens, d = xf.shape
    n_blocks = tokens // tb
    per_seq = seq_len // tb
    width = N_HEADS * HEAD_DV
    cur = lambda a: pl.BlockSpec((tb, a.shape[1]), lambda s: (jnp.minimum(s, n_blocks - 1), 0))
    prev = lambda a: pl.BlockSpec((tb, a.shape[1]), lambda s: (jnp.maximum(s - 1, 0), 0))
    const = lambda a: pl.BlockSpec(a.shape, lambda s: (0,) * a.ndim, pipeline_mode=pl.Buffered(1))
    state = lambda n: pltpu.VMEM((n, HEAD_DV, LANES), F32)
    rows_f32 = lambda w: pltpu.VMEM((tb, w), F32)
    return pl.pallas_call(
        functools.partial(_mixpost_kernel, final, per_seq),
        grid=(n_blocks + 1,),
        in_specs=[cur(a) for a in mix_rows] + [prev(gt), prev(xf)] + [const(a) for a in consts],
        out_specs=pl.BlockSpec((tb, d), lambda s: (jnp.maximum(s - 1, 0), 0)),
        out_shape=jax.ShapeDtypeStruct((tokens, d), F32),
        scratch_shapes=[state(4), state(2), state(2), rows_f32(512), rows_f32(512), rows_f32(256), rows_f32(256),
                        pltpu.VMEM((tb, 3 * width), BF16)],
        compiler_params=pltpu.CompilerParams(
            dimension_semantics=("arbitrary",), vmem_limit_bytes=V7X_SCOPED_VMEM_BYTES),
        name="mixpost",
    )(*mix_rows, gt, xf, *consts)


def _cast_w_in_kernel(w_ref, main_ref, code_ref, gates_ref):
    w = w_ref[0]
    main_ref[0] = w[:, :_C_R].astype(BF16)
    lane = lax.broadcasted_iota(jnp.int32, (w.shape[0], LANES), 1)
    code_ref[0] = jnp.where(lane < GK_RANK, w[:, _C_R:_C_R + LANES], 0.0).astype(BF16)
    gates_ref[0] = w[:, _GATES:].astype(BF16)


def _cast_w_in(w_in, rows):
    depth, d, width = w_in.shape
    n_gates = width - _GATES
    block = lambda wd: pl.BlockSpec((1, rows, wd), lambda l, r: (l, r, 0))
    return pl.pallas_call(
        _cast_w_in_kernel,
        grid=(depth, d // rows),
        in_specs=[block(width)],
        out_specs=[block(_C_R), block(LANES), block(n_gates)],
        out_shape=[jax.ShapeDtypeStruct((depth, d, wd), BF16) for wd in (_C_R, LANES, n_gates)],
        compiler_params=pltpu.CompilerParams(
            dimension_semantics=("parallel", "parallel"), vmem_limit_bytes=V7X_SCOPED_VMEM_BYTES),
        name="cast_w_in",
    )(w_in)


def _rotary_tables(seq_len):
    inv_freq = 1.0 / (ROPE_BASE ** jnp.linspace(0.0, 1.0, 32, dtype=F32))
    ang = jnp.arange(seq_len, dtype=F32)[:, None] * inv_freq[None, :]
    sin, cos = jnp.sin(ang), jnp.cos(ang)
    cos_t = jnp.tile(jnp.repeat(cos, 2, axis=1), (1, N_HEADS))
    sin_t = jnp.tile(jnp.stack([-sin, sin], axis=-1).reshape(seq_len, 64), (1, N_HEADS))
    return cos_t, sin_t


def _retention_tables(cb):
    log_gamma = jnp.log(1.0 - 2.0 ** (-5.0 - jnp.arange(N_HEADS, dtype=F32)))
    pos = jnp.arange(cb, dtype=F32)
    diff = pos[:, None] - pos[None, :]
    dmask = jnp.where(diff >= 0, jnp.exp(diff[None] * log_gamma[:, None, None]), 0.0)
    lane_gamma = jnp.repeat(log_gamma, 64).reshape(N_HEADS // 2, 1, LANES)
    qdec = jnp.exp((pos[None, :, None] + 1.0) * lane_gamma)
    kdec = jnp.exp((cb - 1.0 - pos[None, :, None]) * lane_gamma)
    sdec = jnp.exp(float(cb) * lane_gamma)
    return dmask.reshape(N_HEADS // 2, 2 * cb, cb), qdec, kdec, sdec


def kernel(x, norm_mix, w_in, lb_logits, w_gk_up, b_gk, gn_a, gn_c, w_br_a, w_br_b, w_br_c, w_out,
           norm_ffn, w_ffn_gate, w_ffn_up, w_ffn_down, norm_final):
    batch, seq_len, d = x.shape
    tokens = batch * seq_len
    tm = min(512, seq_len)
    tb = min(256, seq_len)

    w_main, w_code, w_gates = _cast_w_in(w_in, min(256, d))
    wup = jnp.pad(w_gk_up, ((0, 0), (0, LANES - GK_RANK), (0, 0))).astype(BF16)
    wbr = [w.astype(BF16) for w in (w_br_a, w_br_b, w_br_c)]
    wout = w_out.astype(BF16)
    wg, wu, wd = w_ffn_gate.astype(BF16), w_ffn_up.astype(BF16), w_ffn_down.astype(BF16)
    cos_t, sin_t = _rotary_tables(seq_len)
    dmask, qdec, kdec, sdec = _retention_tables(tb)
    pos = jnp.arange(tb)
    tri = ((pos[:, None] >= pos[None, :]) & (pos[:, None] // CHUNK == pos[None, :] // CHUNK)).astype(BF16)
    nfin = norm_final.reshape(1, d)

    xf = x.reshape(tokens, d)
    for layer in range(DEPTH):
        (qa, ka, va, lfa, oga, qb, kb, vb, ogb, qc, kc, vc, lgc, ogc, gt) = _inproj(
            layer, xf, norm_mix[layer].reshape(1, d), w_main[layer], w_code[layer], w_gates[layer], lb_logits,
            cos_t, sin_t, wup[layer], b_gk[layer].reshape(1, -1), seq_len, tm)
        consts = [gn_a[layer].reshape(1, -1), gn_c[layer].reshape(1, -1), tri, dmask, qdec, kdec, sdec,
                  wbr[0][layer], wbr[1][layer], wbr[2][layer], wout[layer], norm_ffn[layer].reshape(1, d),
                  wg[layer], wu[layer], wd[layer], nfin]
        xf = _mixpost(layer == DEPTH - 1, [qa, ka, va, lfa, oga, qb, kb, vb, ogb, qc, kc, vc, lgc, ogc],
                      gt, xf, consts, seq_len, tb)
    return xf.reshape(batch, seq_len, d)
```

```python
import functools

import jax
import jax.numpy as jnp
from jax import lax
from jax.experimental import pallas as pl
from jax.experimental.pallas import tpu as pltpu

F32 = jnp.float32
BF16 = jnp.bfloat16

DEPTH = 4
CHUNK = 64
SUB = 8
NORM_EPS = 1e-6
N_HEADS = 4
HEAD_DV = 128
ROPE_BASE = 10000.0
GK_RANK = 16
GK_NORMALIZER = 16.0
LOG2E = 1.4426950408889634
LANES = 128
V7X_SCOPED_VMEM_BYTES = 60000 * 1024

_A_Q, _A_F, _A_I, _A_G = 0, 512, 1024, 1536
_B_Q, _B_K, _B_V, _B_G = 2048, 2304, 2560, 3072
_C_Q, _C_K, _C_V, _C_G = 3584, 3840, 4096, 4608
_C_R = 5120
_GATES = _C_R + GK_RANK


def _nt(a, b):
    return lax.dot_general(a, b, (((1,), (1,)), ((), ())), preferred_element_type=F32)


def _tn(a, b):
    return lax.dot_general(a, b, (((0,), (0,)), ((), ())), preferred_element_type=F32)


def _mm(a, b):
    return jnp.dot(a, b, preferred_element_type=F32)


def _sigmoid(z):
    return 1.0 / (1.0 + jnp.exp(-z))


def _log_sigmoid(z):
    return jnp.minimum(z, 0.0) - jnp.log1p(jnp.exp(-jnp.abs(z)))


def _rmsnorm(x, w):
    ms = jnp.mean(x * x, axis=-1, keepdims=True)
    return x * lax.rsqrt(ms + NORM_EPS) * w


def _swap_lane_pairs(t):
    n = t.shape[1]
    lane = lax.broadcasted_iota(jnp.int32, t.shape, 1)
    return jnp.where(lane % 2 == 0, pltpu.roll(t, n - 1, 1), pltpu.roll(t, 1, 1))


def _inproj_kernel(layer, x_ref, nw_ref, w_ref, wr_ref, wgt_ref, lbl_ref, cos_ref, sin_ref, wup_ref, bgk_ref,
                   qa_ref, ka_ref, va_ref, lfa_ref, oga_ref,
                   qb_ref, kb_ref, vb_ref, ogb_ref,
                   qc_ref, kc_ref, vc_ref, lgc_ref, ogc_ref, gt_ref):
    xn = _rmsnorm(x_ref[...], nw_ref[...]).astype(BF16)

    def proj(lo, hi):
        return _mm(xn, w_ref[:, lo:hi])

    lbl = lbl_ref[...]
    e = jnp.exp(lbl - jnp.max(lbl, axis=0, keepdims=True))
    p = e / jnp.sum(e, axis=0, keepdims=True)
    c = p[0:1]
    for r in range(1, layer + 1):
        c = c + p[r:r + 1]
    lb = c - p[0:1]

    aq = proj(_A_Q, _A_F)
    qa_ref[...] = (aq * _sigmoid(aq) * (HEAD_DV ** -0.5)).astype(BF16)
    z = proj(_A_F, _A_I)
    t = jnp.exp(-jnp.abs(z))
    pos = z >= 0.0
    log_num = jnp.log(jnp.where(pos, 1.0, lb) + jnp.where(pos, lb, 1.0) * t)
    log_num = jnp.where(jnp.logical_or(pos, lb > 0.0), log_num, z)
    lfa_ref[...] = log_num - jnp.log1p(t)
    ka_ref[...] = ((1.0 - lb) * jnp.where(pos, t, 1.0) / (1.0 + t)).astype(BF16)
    va_ref[...] = proj(_A_I, _A_G).astype(BF16)
    ga = proj(_A_G, _B_Q)
    oga_ref[...] = (ga * _sigmoid(ga)).astype(BF16)

    cos = cos_ref[...]
    sin = sin_ref[...]
    qb = proj(_B_Q, _B_K)
    qb_ref[...] = (qb * cos + _swap_lane_pairs(qb) * sin).astype(BF16)
    kb = proj(_B_K, _B_V)
    kb_ref[...] = ((kb * cos + _swap_lane_pairs(kb) * sin) * (64 ** -0.5)).astype(BF16)
    vb_ref[...] = proj(_B_V, _B_G).astype(BF16)
    gb = proj(_B_G, _C_Q)
    ogb_ref[...] = (gb * _sigmoid(gb)).astype(BF16)

    qc_ref[...] = (proj(_C_Q, _C_K) * (64 ** -0.5)).astype(BF16)
    kc_ref[...] = proj(_C_K, _C_V).astype(BF16)
    vc_ref[...] = proj(_C_V, _C_G).astype(BF16)
    gc = proj(_C_G, _C_R)
    ogc_ref[...] = (gc * _sigmoid(gc)).astype(BF16)
    code = _mm(xn, wr_ref[...]).astype(BF16)
    gk = _mm(code, wup_ref[...]) + bgk_ref[...]
    lgc_ref[...] = _log_sigmoid(gk) * (1.0 / GK_NORMALIZER)

    gt_ref[...] = _mm(xn, wgt_ref[...]).astype(BF16)


def _layer_block(a, layer):
    zeros = (0,) * (a.ndim - 1)
    return pl.BlockSpec((None,) + a.shape[1:], lambda *_: (layer,) + zeros, pipeline_mode=pl.Buffered(1))


def _inproj(layer, xf, nw, w, wr, wgt, lbl, cos_t, sin_t, wup, bgk, seq_len, tm):
    tokens, d = xf.shape
    nt = tokens // tm
    per_seq = seq_len // tm

    def rows(width, dtype):
        return jax.ShapeDtypeStruct((tokens, width), dtype), pl.BlockSpec((tm, width), lambda i: (i, 0))

    outs = [rows(512, BF16), rows(512, BF16), rows(512, BF16), rows(512, F32), rows(512, BF16),
            rows(256, BF16), rows(256, BF16), rows(512, BF16), rows(512, BF16),
            rows(256, BF16), rows(256, BF16), rows(512, BF16), rows(256, F32), rows(512, BF16),
            rows(3072, BF16)]
    const = lambda shape: pl.BlockSpec(shape, lambda i: (0,) * len(shape), pipeline_mode=pl.Buffered(1))
    in_specs = [
        pl.BlockSpec((tm, d), lambda i: (i, 0)),
        const((1, d)),
        _layer_block(w, layer),
        _layer_block(wr, layer),
        _layer_block(wgt, layer),
        const(lbl.shape),
        pl.BlockSpec((tm, 256), lambda i: (i % per_seq, 0)),
        pl.BlockSpec((tm, 256), lambda i: (i % per_seq, 0)),
        _layer_block(wup, layer),
        const(bgk.shape),
    ]
    return pl.pallas_call(
        functools.partial(_inproj_kernel, layer),
        grid=(nt,),
        in_specs=in_specs,
        out_specs=[o[1] for o in outs],
        out_shape=[o[0] for o in outs],
        compiler_params=pltpu.CompilerParams(
            dimension_semantics=("parallel",), vmem_limit_bytes=V7X_SCOPED_VMEM_BYTES),
        name="inproj",
    )(xf, nw, w, wr, wgt, lbl, cos_t, sin_t, wup, bgk)


def _head_masks(nh):
    lane = lax.broadcasted_iota(jnp.int32, (1, LANES), 1)
    return [(lane // (LANES // nh)) == m for m in range(nh)]


def _stack_heads(x, masks):
    if len(masks) == 1:
        return x
    return jnp.concatenate([jnp.where(m, x, 0.0) for m in masks], axis=0)


def _cumsum_rows_f32(tri, g):
    hi = g.astype(BF16)
    r1 = g - hi.astype(F32)
    mid = r1.astype(BF16)
    lo = (r1 - mid.astype(F32)).astype(BF16)
    return _mm(tri, hi) + _mm(tri, mid) + _mm(tri, lo)


def _gla_chunk_scores(qs, bs, kf_sc, b_sc, r0, masks):
    ng, nh = len(qs), len(masks)
    lane_c = lax.broadcasted_iota(jnp.int32, (SUB, CHUNK), 1)
    row_c = lax.broadcasted_iota(jnp.int32, (SUB, CHUNK), 0)
    blocks = [[[] for _ in range(nh)] for _ in range(ng)]
    kk_rows = [[] for _ in range(ng)]
    for blk in range(CHUNK // SUB):
        lo = blk * SUB
        causal = (lane_c - lo) <= row_c
        dg = [[jnp.zeros((SUB, CHUNK), F32) for _ in range(nh)] for _ in range(ng)]
        for jj in range(SUB):
            j = lo + jj
            hit = lane_c == j
            for gi in range(ng):
                lanes = slice(gi * LANES, (gi + 1) * LANES)
                kj = kf_sc[r0 + j:r0 + j + 1, lanes]
                bj = b_sc[r0 + j:r0 + j + 1, lanes]
                pr = qs[gi][lo:lo + SUB] * kj * jnp.exp2(bs[gi][lo:lo + SUB] - bj)
                for m in range(nh):
                    pm = pr if nh == 1 else jnp.where(masks[m], pr, 0.0)
                    dg[gi][m] = jnp.where(hit, jnp.sum(pm, axis=-1, keepdims=True), dg[gi][m])
        for gi in range(ng):
            lanes = slice(gi * LANES, (gi + 1) * LANES)
            blk_dg = [jnp.where(causal, d, 0.0) for d in dg[gi]]
            if blk > 0:
                ref_row = b_sc[r0 + lo - 1:r0 + lo, lanes]
                qp = qs[gi][lo:lo + SUB] * jnp.exp2(bs[gi][lo:lo + SUB] - ref_row)
                if blk > 1:
                    step = jnp.exp2(ref_row - b_sc[r0 + lo - SUB - 1:r0 + lo - SUB, lanes])
                    kk_rows[gi] = [kr * step for kr in kk_rows[gi]]
                kk_rows[gi].append(kf_sc[r0 + lo - SUB:r0 + lo, lanes]
                                   * jnp.exp2(ref_row - bs[gi][lo - SUB:lo]))
                kk = jnp.concatenate(kk_rows[gi] + [jnp.zeros((CHUNK - lo, LANES), F32)], axis=0).astype(BF16)
                lhs = _stack_heads(qp, masks)
                if lhs.shape[0] < 16:
                    lhs = jnp.concatenate([lhs, jnp.zeros((16 - lhs.shape[0], LANES), F32)], axis=0)
                off = _nt(lhs.astype(BF16), kk)
                blk_dg = [off[m * SUB:(m + 1) * SUB] + blk_dg[m] for m in range(nh)]
            for m in range(nh):
                blocks[gi][m].append(blk_dg[m])
    return [[jnp.concatenate(bl, axis=0) for bl in grp] for grp in blocks]


def _gla_stage(nh, q_ref, k_ref, v_ref, g_ref, og_ref, gn_ref, tri_ref, o_ref, o_base, st_ref, b_sc, kf_sc):
    tb = q_ref.shape[0]
    n_groups = q_ref.shape[1] // LANES
    masks = _head_masks(nh)
    gn = gn_ref[...]
    b_sc[...] = _cumsum_rows_f32(tri_ref[...], g_ref[...]) * LOG2E
    kf_sc[...] = k_ref[...].astype(F32)

    for c in range(tb // CHUNK):
        r0 = c * CHUNK
        rows = slice(r0, r0 + CHUNK)
        qs = [q_ref[rows, gi * LANES:(gi + 1) * LANES].astype(F32) for gi in range(n_groups)]
        bs = [b_sc[rows, gi * LANES:(gi + 1) * LANES] for gi in range(n_groups)]
        scores = _gla_chunk_scores(qs, bs, kf_sc, b_sc, r0, masks)
        for gi in range(n_groups):
            lanes = slice(gi * LANES, (gi + 1) * LANES)
            b = bs[gi]
            b_last = b_sc[r0 + CHUNK - 1:r0 + CHUNK, lanes]
            st = st_ref[gi]
            inter = _nt(_stack_heads(qs[gi] * jnp.exp2(b), masks).astype(BF16), st.astype(BF16))
            vs = []
            for m in range(nh):
                vl = slice((gi * nh + m) * HEAD_DV, (gi * nh + m + 1) * HEAD_DV)
                v = v_ref[rows, vl]
                vs.append(v)
                o = _mm(scores[gi][m].astype(BF16), v) + inter[m * CHUNK:(m + 1) * CHUNK]
                y = _rmsnorm(o, gn) * og_ref[rows, vl].astype(F32)
                o_ref[rows, o_base + vl.start:o_base + vl.stop] = y.astype(BF16)
            kd = _stack_heads(kf_sc[rows, lanes] * jnp.exp2(b_last - b), masks).astype(BF16)
            vcat = vs[0] if nh == 1 else jnp.concatenate(vs, axis=0)
            st_ref[gi] = st * jnp.exp2(b_last) + _tn(vcat, kd)


def _ret_stage(q_ref, k_ref, v_ref, og_ref, dmask_ref, qdec_ref, kdec_ref, sdec_ref, o_ref, o_base, st_ref):
    nh = 2
    cb = q_ref.shape[0]
    masks = _head_masks(nh)
    for gi in range(q_ref.shape[1] // LANES):
        lanes = slice(gi * LANES, (gi + 1) * LANES)
        q = q_ref[:, lanes].astype(F32)
        k = k_ref[:, lanes].astype(F32)
        sc = _nt(_stack_heads(q, masks).astype(BF16), k_ref[:, lanes]) * dmask_ref[gi]
        st = st_ref[gi]
        inter = _nt(_stack_heads(q * qdec_ref[gi], masks).astype(BF16), st.astype(BF16))
        vs = []
        for m in range(nh):
            vl = slice((gi * nh + m) * HEAD_DV, (gi * nh + m + 1) * HEAD_DV)
            v = v_ref[:, vl]
            vs.append(v)
            o = _mm(sc[m * cb:(m + 1) * cb].astype(BF16), v) + inter[m * cb:(m + 1) * cb]
            mu = jnp.mean(o, axis=-1, keepdims=True)
            var = jnp.mean(jnp.square(o - mu), axis=-1, keepdims=True)
            y = (o - mu) * lax.rsqrt(var + NORM_EPS) * og_ref[:, vl].astype(F32)
            o_ref[:, o_base + vl.start:o_base + vl.stop] = y.astype(BF16)
        kd = _stack_heads(k * kdec_ref[gi], masks).astype(BF16)
        st_ref[gi] = st * sdec_ref[gi] + _tn(jnp.concatenate(vs, axis=0), kd)


def _post_stage(final, o_branches, gt_ref, x_ref, wbr_refs, wout_ref, nffn_ref, wg_ref, wu_ref, wd_ref,
                nfin_ref, out_ref):
    d = x_ref.shape[1]
    merged = None
    for i, o_branch in enumerate(o_branches):
        gate = _sigmoid(gt_ref[:, i * d:(i + 1) * d].astype(F32))
        term = gate * _mm(o_branch, wbr_refs[i][...])
        merged = term if merged is None else merged + term
    h = x_ref[...] + _mm(merged.astype(BF16), wout_ref[...])
    hn = _rmsnorm(h, nffn_ref[...]).astype(BF16)
    g = _mm(hn, wg_ref[...])
    u = _mm(hn, wu_ref[...])
    y = h + _mm((g * _sigmoid(g) * u).astype(BF16), wd_ref[...])
    if final:
        y = _rmsnorm(y, nfin_ref[...])
    out_ref[...] = y


def _mixpost_kernel(final, per_seq,
                    qa_ref, ka_ref, va_ref, lfa_ref, oga_ref, qb_ref, kb_ref, vb_ref, ogb_ref,
                    qc_ref, kc_ref, vc_ref, lgc_ref, ogc_ref, gt_ref, x_ref,
                    gna_ref, gnc_ref, tri_ref, dmask_ref, qdec_ref, kdec_ref, sdec_ref, nffn_ref, nfin_ref,
                    wbra_ref, wbrb_ref, wbrc_ref, wout_ref, wg_ref, wu_ref, wd_ref,
                    out_ref,
                    sta_ref, stb_ref, stc_ref, ba_sc, kfa_sc, bc_sc, kfc_sc, o_sc):
    s = pl.program_id(0)

    @pl.when(s % per_seq == 0)
    def _():
        sta_ref[...] = jnp.zeros_like(sta_ref)
        stb_ref[...] = jnp.zeros_like(stb_ref)
        stc_ref[...] = jnp.zeros_like(stc_ref)

    @pl.when(s == 0)
    def _():
        o_sc[...] = jnp.zeros_like(o_sc)

    width = N_HEADS * HEAD_DV
    o_prev = [o_sc[:, i * width:(i + 1) * width] for i in range(3)]
    _gla_stage(1, qa_ref, ka_ref, va_ref, lfa_ref, oga_ref, gna_ref, tri_ref, o_sc, 0, sta_ref, ba_sc, kfa_sc)
    _ret_stage(qb_ref, kb_ref, vb_ref, ogb_ref, dmask_ref, qdec_ref, kdec_ref, sdec_ref, o_sc, width,
               stb_ref)
    _gla_stage(2, qc_ref, kc_ref, vc_ref, lgc_ref, ogc_ref, gnc_ref, tri_ref, o_sc, 2 * width, stc_ref,
               bc_sc, kfc_sc)
    _post_stage(final, o_prev, gt_ref, x_ref, (wbra_ref, wbrb_ref, wbrc_ref), wout_ref, nffn_ref, wg_ref,
                wu_ref, wd_ref, nfin_ref, out_ref)


def _mixpost(layer, mix_rows, gt, xf, consts, weights, seq_len, tb):
    tokens, d = xf.shape
    n_blocks = tokens // tb
    per_seq = seq_len // tb
    width = N_HEADS * HEAD_DV
    cur = lambda a: pl.BlockSpec((tb, a.shape[1]), lambda s: (jnp.minimum(s, n_blocks - 1), 0))
    prev = lambda a: pl.BlockSpec((tb, a.shape[1]), lambda s: (jnp.maximum(s - 1, 0), 0))
    const = lambda a: pl.BlockSpec(a.shape, lambda s: (0,) * a.ndim, pipeline_mode=pl.Buffered(1))
    state = lambda n: pltpu.VMEM((n, HEAD_DV, LANES), F32)
    rows_f32 = lambda w: pltpu.VMEM((tb, w), F32)
    return pl.pallas_call(
        functools.partial(_mixpost_kernel, layer == DEPTH - 1, per_seq),
        grid=(n_blocks + 1,),
        in_specs=([cur(a) for a in mix_rows] + [prev(gt), prev(xf)] + [const(a) for a in consts]
                  + [_layer_block(a, layer) for a in weights]),
        out_specs=pl.BlockSpec((tb, d), lambda s: (jnp.maximum(s - 1, 0), 0)),
        out_shape=jax.ShapeDtypeStruct((tokens, d), F32),
        scratch_shapes=[state(4), state(2), state(2), rows_f32(512), rows_f32(512), rows_f32(256), rows_f32(256),
                        pltpu.VMEM((tb, 3 * width), BF16)],
        compiler_params=pltpu.CompilerParams(
            dimension_semantics=("arbitrary",), vmem_limit_bytes=V7X_SCOPED_VMEM_BYTES),
        name="mixpost",
    )(*mix_rows, gt, xf, *consts, *weights)


def _rotary_tables(seq_len):
    inv_freq = 1.0 / (ROPE_BASE ** jnp.linspace(0.0, 1.0, 32, dtype=F32))
    ang = jnp.arange(seq_len, dtype=F32)[:, None] * inv_freq[None, :]
    sin, cos = jnp.sin(ang), jnp.cos(ang)
    cos_t = jnp.tile(jnp.repeat(cos, 2, axis=1), (1, N_HEADS))
    sin_t = jnp.tile(jnp.stack([-sin, sin], axis=-1).reshape(seq_len, 64), (1, N_HEADS))
    return cos_t, sin_t


def _retention_tables(cb):
    log_gamma = jnp.log(1.0 - 2.0 ** (-5.0 - jnp.arange(N_HEADS, dtype=F32)))
    pos = jnp.arange(cb, dtype=F32)
    diff = pos[:, None] - pos[None, :]
    dmask = jnp.where(diff >= 0, jnp.exp(diff[None] * log_gamma[:, None, None]), 0.0)
    lane_gamma = jnp.repeat(log_gamma, 64).reshape(N_HEADS // 2, 1, LANES)
    qdec = jnp.exp((pos[None, :, None] + 1.0) * lane_gamma)
    kdec = jnp.exp((cb - 1.0 - pos[None, :, None]) * lane_gamma)
    sdec = jnp.exp(float(cb) * lane_gamma)
    return dmask.reshape(N_HEADS // 2, 2 * cb, cb), qdec, kdec, sdec


def kernel(x, norm_mix, w_in, lb_logits, w_gk_up, b_gk, gn_a, gn_c, w_br_a, w_br_b, w_br_c, w_out,
           norm_ffn, w_ffn_gate, w_ffn_up, w_ffn_down, norm_final):
    batch, seq_len, d = x.shape
    tokens = batch * seq_len
    tm = min(512, seq_len)
    tb = min(256, seq_len)

    w_main = w_in[:, :, :_C_R].astype(BF16)
    w_code = jnp.pad(w_in[:, :, _C_R:_GATES], ((0, 0), (0, 0), (0, LANES - GK_RANK))).astype(BF16)
    w_gates = w_in[:, :, _GATES:].astype(BF16)
    wup = jnp.pad(w_gk_up, ((0, 0), (0, LANES - GK_RANK), (0, 0))).astype(BF16)
    wbr = [w.astype(BF16) for w in (w_br_a, w_br_b, w_br_c)]
    wout = w_out.astype(BF16)
    wg, wu, wd = w_ffn_gate.astype(BF16), w_ffn_up.astype(BF16), w_ffn_down.astype(BF16)
    cos_t, sin_t = _rotary_tables(seq_len)
    dmask, qdec, kdec, sdec = _retention_tables(tb)
    pos = jnp.arange(tb)
    tri = ((pos[:, None] >= pos[None, :]) & (pos[:, None] // CHUNK == pos[None, :] // CHUNK)).astype(BF16)
    nfin = norm_final.reshape(1, d)

    xf = x.reshape(tokens, d)
    for layer in range(DEPTH):
        (qa, ka, va, lfa, oga, qb, kb, vb, ogb, qc, kc, vc, lgc, ogc, gt) = _inproj(
            layer, xf, norm_mix[layer].reshape(1, d), w_main, w_code, w_gates, lb_logits,
            cos_t, sin_t, wup, b_gk[layer].reshape(1, -1), seq_len, tm)
        consts = [gn_a[layer].reshape(1, -1), gn_c[layer].reshape(1, -1), tri, dmask, qdec, kdec, sdec,
                  norm_ffn[layer].reshape(1, d), nfin]
        xf = _mixpost(layer, [qa, ka, va, lfa, oga, qb, kb, vb, ogb, qc, kc, vc, lgc, ogc],
                      gt, xf, consts, wbr + [wout, wg, wu, wd], seq_len, tb)
    return xf.reshape(batch, seq_len, d)
```

```python
import functools

import jax
import jax.numpy as jnp
from jax import lax
from jax.experimental import pallas as pl
from jax.experimental.pallas import tpu as pltpu

F32 = jnp.float32
BF16 = jnp.bfloat16

DEPTH = 4
CHUNK = 64
SUB = 8
NORM_EPS = 1e-6
N_HEADS = 4
HEAD_DV = 128
QK_WIDE = 128
QK_NARROW = 64
ROPE_BASE = 10000.0
GK_RANK = 16
GK_NORMALIZER = 16.0
LOG2E = 1.4426950408889634
LANES = 128
V7X_SCOPED_VMEM_BYTES = 60000 * 1024

_A_Q, _A_F, _A_I, _A_G = 0, 512, 1024, 1536
_B_Q, _B_K, _B_V, _B_G = 2048, 2304, 2560, 3072
_C_Q, _C_K, _C_V, _C_G = 3584, 3840, 4096, 4608
_C_R = 5120
_GATES = _C_R + GK_RANK


def _nt(a, b):
    return lax.dot_general(a, b, (((1,), (1,)), ((), ())), preferred_element_type=F32)


def _tn(a, b):
    return lax.dot_general(a, b, (((0,), (0,)), ((), ())), preferred_element_type=F32)


def _mm(a, b):
    return jnp.dot(a, b, preferred_element_type=F32)


def _sigmoid(z):
    return 0.5 * jnp.tanh(0.5 * z) + 0.5


def _log_sigmoid(z):
    return jnp.minimum(z, 0.0) - jnp.log1p(jnp.exp(-jnp.abs(z)))


def _rmsnorm(x, w):
    ms = jnp.mean(x * x, axis=-1, keepdims=True)
    return x * lax.rsqrt(ms + NORM_EPS) * w


def _swap_lane_pairs(t):
    n = t.shape[1]
    lane = lax.broadcasted_iota(jnp.int32, t.shape, 1)
    return jnp.where(lane % 2 == 0, pltpu.roll(t, n - 1, 1), pltpu.roll(t, 1, 1))


def _inproj_kernel(layer, x_ref, nw_ref, w_ref, wr_ref, wgt_ref, lbl_ref, cos_ref, sin_ref, wup_ref, bgk_ref,
                   qa_ref, ka_ref, va_ref, lfa_ref, oga_ref,
                   qb_ref, kb_ref, vb_ref, ogb_ref,
                   qc_ref, kc_ref, vc_ref, lgc_ref, ogc_ref, gt_ref):
    xn = _rmsnorm(x_ref[...], nw_ref[...]).astype(BF16)

    def proj(lo, hi):
        return _mm(xn, w_ref[:, lo:hi])

    lbl = lbl_ref[...]
    e = jnp.exp(lbl - jnp.max(lbl, axis=0, keepdims=True))
    p = e / jnp.sum(e, axis=0, keepdims=True)
    c = p[0:1]
    for r in range(1, layer + 1):
        c = c + p[r:r + 1]
    lb = c - p[0:1]

    aq = proj(_A_Q, _A_F)
    qa_ref[...] = (aq * _sigmoid(aq) * (QK_WIDE ** -0.5)).astype(BF16)
    z = proj(_A_F, _A_I)
    t = jnp.exp(-jnp.abs(z))
    pos = z >= 0.0
    log_num = jnp.log(jnp.where(pos, 1.0, lb) + jnp.where(pos, lb, 1.0) * t)
    log_num = jnp.where(jnp.logical_or(pos, lb > 0.0), log_num, z)
    lfa_ref[...] = log_num - jnp.log1p(t)
    ka_ref[...] = ((1.0 - lb) * jnp.where(pos, t, 1.0) / (1.0 + t)).astype(BF16)
    va_ref[...] = proj(_A_I, _A_G).astype(BF16)
    ga = proj(_A_G, _B_Q)
    oga_ref[...] = (ga * _sigmoid(ga)).astype(BF16)

    cos = cos_ref[...]
    sin = sin_ref[...]
    qb = proj(_B_Q, _B_K)
    qb_ref[...] = (qb * cos + _swap_lane_pairs(qb) * sin).astype(BF16)
    kb = proj(_B_K, _B_V)
    kb_ref[...] = ((kb * cos + _swap_lane_pairs(kb) * sin) * (QK_NARROW ** -0.5)).astype(BF16)
    vb_ref[...] = proj(_B_V, _B_G).astype(BF16)
    gb = proj(_B_G, _C_Q)
    ogb_ref[...] = (gb * _sigmoid(gb)).astype(BF16)

    qc_ref[...] = (proj(_C_Q, _C_K) * (QK_NARROW ** -0.5)).astype(BF16)
    kc_ref[...] = proj(_C_K, _C_V).astype(BF16)
    vc_ref[...] = proj(_C_V, _C_G).astype(BF16)
    gc = proj(_C_G, _C_R)
    ogc_ref[...] = (gc * _sigmoid(gc)).astype(BF16)
    code = _mm(xn, wr_ref[...]).astype(BF16)
    gk = _mm(code, wup_ref[...]) + bgk_ref[...]
    lgc_ref[...] = _log_sigmoid(gk) * (1.0 / GK_NORMALIZER)

    gt_ref[...] = _mm(xn, wgt_ref[...]).astype(BF16)


def _layer_block(a, layer):
    zeros = (0,) * (a.ndim - 1)
    return pl.BlockSpec((None,) + a.shape[1:], lambda *_: (layer,) + zeros, pipeline_mode=pl.Buffered(1))


def _inproj(layer, xf, nw, w, wr, wgt, lbl, cos_t, sin_t, wup, bgk, seq_len, tm):
    tokens, d = xf.shape
    nt = tokens // tm
    per_seq = seq_len // tm

    def rows(width, dtype):
        return jax.ShapeDtypeStruct((tokens, width), dtype), pl.BlockSpec((tm, width), lambda i: (i, 0))

    wide, narrow = N_HEADS * HEAD_DV, N_HEADS * QK_NARROW
    outs = [rows(wide, BF16), rows(wide, BF16), rows(wide, BF16), rows(wide, F32), rows(wide, BF16),
            rows(narrow, BF16), rows(narrow, BF16), rows(wide, BF16), rows(wide, BF16),
            rows(narrow, BF16), rows(narrow, BF16), rows(wide, BF16), rows(narrow, F32), rows(wide, BF16),
            rows(wgt.shape[-1], BF16)]
    const = lambda shape: pl.BlockSpec(shape, lambda i: (0,) * len(shape), pipeline_mode=pl.Buffered(1))
    in_specs = [
        pl.BlockSpec((tm, d), lambda i: (i, 0)),
        const((1, d)),
        _layer_block(w, layer),
        _layer_block(wr, layer),
        _layer_block(wgt, layer),
        const(lbl.shape),
        pl.BlockSpec((tm, narrow), lambda i: (i % per_seq, 0)),
        pl.BlockSpec((tm, narrow), lambda i: (i % per_seq, 0)),
        _layer_block(wup, layer),
        const(bgk.shape),
    ]
    return pl.pallas_call(
        functools.partial(_inproj_kernel, layer),
        grid=(nt,),
        in_specs=in_specs,
        out_specs=[o[1] for o in outs],
        out_shape=[o[0] for o in outs],
        compiler_params=pltpu.CompilerParams(
            dimension_semantics=("parallel",), vmem_limit_bytes=V7X_SCOPED_VMEM_BYTES),
        name="inproj",
    )(xf, nw, w, wr, wgt, lbl, cos_t, sin_t, wup, bgk)


def _head_masks(nh):
    lane = lax.broadcasted_iota(jnp.int32, (1, LANES), 1)
    return [(lane // (LANES // nh)) == m for m in range(nh)]


def _stack_heads(x, masks):
    if len(masks) == 1:
        return x
    return jnp.concatenate([jnp.where(m, x, 0.0) for m in masks], axis=0)


def _cumsum_rows_f32(tri, g):
    hi = g.astype(BF16)
    r1 = g - hi.astype(F32)
    mid = r1.astype(BF16)
    lo = (r1 - mid.astype(F32)).astype(BF16)
    return _mm(tri, hi) + _mm(tri, mid) + _mm(tri, lo)


def _gla_chunk_scores(qs, bs, kf_sc, b_sc, r0, masks):
    ng, nh = len(qs), len(masks)
    lane_c = lax.broadcasted_iota(jnp.int32, (SUB, CHUNK), 1)
    row_c = lax.broadcasted_iota(jnp.int32, (SUB, CHUNK), 0)
    blocks = [[[] for _ in range(nh)] for _ in range(ng)]
    kk_rows = [[] for _ in range(ng)]
    for blk in range(CHUNK // SUB):
        lo = blk * SUB
        causal = (lane_c - lo) <= row_c
        dg = [[jnp.zeros((SUB, CHUNK), F32) for _ in range(nh)] for _ in range(ng)]
        for jj in range(SUB):
            j = lo + jj
            hit = lane_c == j
            for gi in range(ng):
                lanes = slice(gi * LANES, (gi + 1) * LANES)
                kj = kf_sc[r0 + j:r0 + j + 1, lanes]
                bj = b_sc[r0 + j:r0 + j + 1, lanes]
                pr = qs[gi][lo:lo + SUB] * kj * jnp.exp2(bs[gi][lo:lo + SUB] - bj)
                for m in range(nh):
                    pm = pr if nh == 1 else jnp.where(masks[m], pr, 0.0)
                    dg[gi][m] = jnp.where(hit, jnp.sum(pm, axis=-1, keepdims=True), dg[gi][m])
        for gi in range(ng):
            lanes = slice(gi * LANES, (gi + 1) * LANES)
            blk_dg = [jnp.where(causal, d, 0.0) for d in dg[gi]]
            if blk > 0:
                ref_row = b_sc[r0 + lo - 1:r0 + lo, lanes]
                qp = qs[gi][lo:lo + SUB] * jnp.exp2(bs[gi][lo:lo + SUB] - ref_row)
                if blk > 1:
                    step = jnp.exp2(ref_row - b_sc[r0 + lo - SUB - 1:r0 + lo - SUB, lanes])
                    kk_rows[gi] = [kr * step for kr in kk_rows[gi]]
                kk_rows[gi].append(kf_sc[r0 + lo - SUB:r0 + lo, lanes]
                                   * jnp.exp2(ref_row - bs[gi][lo - SUB:lo]))
                kk = jnp.concatenate(kk_rows[gi] + [jnp.zeros((CHUNK - lo, LANES), F32)], axis=0).astype(BF16)
                lhs = _stack_heads(qp, masks)
                if lhs.shape[0] < 16:
                    lhs = jnp.concatenate([lhs, jnp.zeros((16 - lhs.shape[0], LANES), F32)], axis=0)
                off = _nt(lhs.astype(BF16), kk)
                blk_dg = [off[m * SUB:(m + 1) * SUB] + blk_dg[m] for m in range(nh)]
            for m in range(nh):
                blocks[gi][m].append(blk_dg[m])
    return [[jnp.concatenate(bl, axis=0) for bl in grp] for grp in blocks]


def _gla_stage(nh, q_ref, k_ref, v_ref, g_ref, og_ref, gn_ref, tri_ref, o_ref, o_base, st_ref, b_sc, kf_sc):
    tb = q_ref.shape[0]
    n_groups = q_ref.shape[1] // LANES
    masks = _head_masks(nh)
    gn = gn_ref[...]
    b_sc[...] = _cumsum_rows_f32(tri_ref[...], g_ref[...]) * LOG2E
    kf_sc[...] = k_ref[...].astype(F32)

    for c in range(tb // CHUNK):
        r0 = c * CHUNK
        rows = slice(r0, r0 + CHUNK)
        qs = [q_ref[rows, gi * LANES:(gi + 1) * LANES].astype(F32) for gi in range(n_groups)]
        bs = [b_sc[rows, gi * LANES:(gi + 1) * LANES] for gi in range(n_groups)]
        scores = _gla_chunk_scores(qs, bs, kf_sc, b_sc, r0, masks)
        for gi in range(n_groups):
            lanes = slice(gi * LANES, (gi + 1) * LANES)
            b = bs[gi]
            b_last = b_sc[r0 + CHUNK - 1:r0 + CHUNK, lanes]
            st = st_ref[gi]
            inter = _nt(_stack_heads(qs[gi] * jnp.exp2(b), masks).astype(BF16), st.astype(BF16))
            vs = []
            for m in range(nh):
                vl = slice((gi * nh + m) * HEAD_DV, (gi * nh + m + 1) * HEAD_DV)
                v = v_ref[rows, vl]
                vs.append(v)
                o = _mm(scores[gi][m].astype(BF16), v) + inter[m * CHUNK:(m + 1) * CHUNK]
                y = _rmsnorm(o, gn) * og_ref[rows, vl].astype(F32)
                o_ref[rows, o_base + vl.start:o_base + vl.stop] = y.astype(BF16)
            kd = _stack_heads(kf_sc[rows, lanes] * jnp.exp2(b_last - b), masks).astype(BF16)
            vcat = vs[0] if nh == 1 else jnp.concatenate(vs, axis=0)
            st_ref[gi] = st * jnp.exp2(b_last) + _tn(vcat, kd)


def _ret_stage(q_ref, k_ref, v_ref, og_ref, dmask_ref, qdec_ref, kdec_ref, sdec_ref, o_ref, o_base, st_ref):
    nh = 2
    cb = q_ref.shape[0]
    masks = _head_masks(nh)
    for gi in range(q_ref.shape[1] // LANES):
        lanes = slice(gi * LANES, (gi + 1) * LANES)
        q = q_ref[:, lanes].astype(F32)
        k = k_ref[:, lanes].astype(F32)
        sc = _nt(_stack_heads(q, masks).astype(BF16), k_ref[:, lanes]) * dmask_ref[gi]
        st = st_ref[gi]
        inter = _nt(_stack_heads(q * qdec_ref[gi], masks).astype(BF16), st.astype(BF16))
        vs = []
        for m in range(nh):
            vl = slice((gi * nh + m) * HEAD_DV, (gi * nh + m + 1) * HEAD_DV)
            v = v_ref[:, vl]
            vs.append(v)
            o = _mm(sc[m * cb:(m + 1) * cb].astype(BF16), v) + inter[m * cb:(m + 1) * cb]
            mu = jnp.mean(o, axis=-1, keepdims=True)
            var = jnp.mean(jnp.square(o - mu), axis=-1, keepdims=True)
            y = (o - mu) * lax.rsqrt(var + NORM_EPS) * og_ref[:, vl].astype(F32)
            o_ref[:, o_base + vl.start:o_base + vl.stop] = y.astype(BF16)
        kd = _stack_heads(k * kdec_ref[gi], masks).astype(BF16)
        st_ref[gi] = st * sdec_ref[gi] + _tn(jnp.concatenate(vs, axis=0), kd)


def _post_stage(final, o_branches, gt_ref, x_ref, wbr_refs, wout_ref, nffn_ref, wg_ref, wu_ref, wd_ref,
                nfin_ref, out_ref):
    d = x_ref.shape[1]
    merged = None
    for i, o_branch in enumerate(o_branches):
        gate = _sigmoid(gt_ref[:, i * d:(i + 1) * d].astype(F32))
        term = gate * _mm(o_branch, wbr_refs[i][...])
        merged = term if merged is None else merged + term
    h = x_ref[...] + _mm(merged.astype(BF16), wout_ref[...])
    hn = _rmsnorm(h, nffn_ref[...]).astype(BF16)
    g = _mm(hn, wg_ref[...])
    u = _mm(hn, wu_ref[...])
    y = h + _mm((g * _sigmoid(g) * u).astype(BF16), wd_ref[...])
    if final:
        y = _rmsnorm(y, nfin_ref[...])
    out_ref[...] = y


def _mixpost_kernel(final, per_seq,
                    qa_ref, ka_ref, va_ref, lfa_ref, oga_ref, qb_ref, kb_ref, vb_ref, ogb_ref,
                    qc_ref, kc_ref, vc_ref, lgc_ref, ogc_ref, gt_ref, x_ref,
                    gna_ref, gnc_ref, tri_ref, dmask_ref, qdec_ref, kdec_ref, sdec_ref, nffn_ref, nfin_ref,
                    wbra_ref, wbrb_ref, wbrc_ref, wout_ref, wg_ref, wu_ref, wd_ref,
                    out_ref,
                    sta_ref, stb_ref, stc_ref, ba_sc, kfa_sc, bc_sc, kfc_sc, o_sc):
    s = pl.program_id(0)

    @pl.when(s % per_seq == 0)
    def _():
        sta_ref[...] = jnp.zeros_like(sta_ref)
        stb_ref[...] = jnp.zeros_like(stb_ref)
        stc_ref[...] = jnp.zeros_like(stc_ref)

    @pl.when(s == 0)
    def _():
        o_sc[...] = jnp.zeros_like(o_sc)

    width = N_HEADS * HEAD_DV
    o_prev = [o_sc[:, i * width:(i + 1) * width] for i in range(3)]
    _gla_stage(1, qa_ref, ka_ref, va_ref, lfa_ref, oga_ref, gna_ref, tri_ref, o_sc, 0, sta_ref, ba_sc, kfa_sc)
    _ret_stage(qb_ref, kb_ref, vb_ref, ogb_ref, dmask_ref, qdec_ref, kdec_ref, sdec_ref, o_sc, width,
               stb_ref)
    _gla_stage(2, qc_ref, kc_ref, vc_ref, lgc_ref, ogc_ref, gnc_ref, tri_ref, o_sc, 2 * width, stc_ref,
               bc_sc, kfc_sc)
    _post_stage(final, o_prev, gt_ref, x_ref, (wbra_ref, wbrb_ref, wbrc_ref), wout_ref, nffn_ref, wg_ref,
                wu_ref, wd_ref, nfin_ref, out_ref)


def _mixpost(layer, mix_rows, gt, xf, consts, weights, seq_len, tb):
    tokens, d = xf.shape
    n_blocks = tokens // tb
    per_seq = seq_len // tb
    width = N_HEADS * HEAD_DV
    cur = lambda a: pl.BlockSpec((tb, a.shape[1]), lambda s: (jnp.minimum(s, n_blocks - 1), 0))
    prev = lambda a: pl.BlockSpec((tb, a.shape[1]), lambda s: (jnp.maximum(s - 1, 0), 0))
    const = lambda a: pl.BlockSpec(a.shape, lambda s: (0,) * a.ndim, pipeline_mode=pl.Buffered(1))
    state = lambda n: pltpu.VMEM((n, HEAD_DV, LANES), F32)
    rows_f32 = lambda w: pltpu.VMEM((tb, w), F32)
    return pl.pallas_call(
        functools.partial(_mixpost_kernel, layer == DEPTH - 1, per_seq),
        grid=(n_blocks + 1,),
        in_specs=([cur(a) for a in mix_rows] + [prev(gt), prev(xf)] + [const(a) for a in consts]
                  + [_layer_block(a, layer) for a in weights]),
        out_specs=pl.BlockSpec((tb, d), lambda s: (jnp.maximum(s - 1, 0), 0)),
        out_shape=jax.ShapeDtypeStruct((tokens, d), F32),
        scratch_shapes=[state(4), state(2), state(2), rows_f32(512), rows_f32(512), rows_f32(256), rows_f32(256),
                        pltpu.VMEM((tb, 3 * width), BF16)],
        compiler_params=pltpu.CompilerParams(
            dimension_semantics=("arbitrary",), vmem_limit_bytes=V7X_SCOPED_VMEM_BYTES),
        name="mixpost",
    )(*mix_rows, gt, xf, *consts, *weights)


def _rotary_tables(seq_len):
    inv_freq = 1.0 / (ROPE_BASE ** jnp.linspace(0.0, 1.0, QK_NARROW // 2, dtype=F32))
    ang = jnp.arange(seq_len, dtype=F32)[:, None] * inv_freq[None, :]
    sin, cos = jnp.sin(ang), jnp.cos(ang)
    cos_t = jnp.tile(jnp.repeat(cos, 2, axis=1), (1, N_HEADS))
    sin_t = jnp.tile(jnp.stack([-sin, sin], axis=-1).reshape(seq_len, QK_NARROW), (1, N_HEADS))
    return cos_t, sin_t


def _retention_tables(cb):
    log_gamma = jnp.log(1.0 - 2.0 ** (-5.0 - jnp.arange(N_HEADS, dtype=F32)))
    pos = jnp.arange(cb, dtype=F32)
    diff = pos[:, None] - pos[None, :]
    dmask = jnp.where(diff >= 0, jnp.exp(diff[None] * log_gamma[:, None, None]), 0.0)
    lane_gamma = jnp.repeat(log_gamma, QK_NARROW).reshape(N_HEADS // 2, 1, LANES)
    qdec = jnp.exp((pos[None, :, None] + 1.0) * lane_gamma)
    kdec = jnp.exp((cb - 1.0 - pos[None, :, None]) * lane_gamma)
    sdec = jnp.exp(float(cb) * lane_gamma)
    return dmask.reshape(N_HEADS // 2, 2 * cb, cb), qdec, kdec, sdec


def kernel(x, norm_mix, w_in, lb_logits, w_gk_up, b_gk, gn_a, gn_c, w_br_a, w_br_b, w_br_c, w_out,
           norm_ffn, w_ffn_gate, w_ffn_up, w_ffn_down, norm_final):
    batch, seq_len, d = x.shape
    tokens = batch * seq_len
    tm = min(512, seq_len)
    tb = min(256, seq_len)

    w_main = w_in[:, :, :_C_R].astype(BF16)
    w_code = jnp.pad(w_in[:, :, _C_R:_GATES], ((0, 0), (0, 0), (0, LANES - GK_RANK))).astype(BF16)
    w_gates = w_in[:, :, _GATES:].astype(BF16)
    wup = jnp.pad(w_gk_up, ((0, 0), (0, LANES - GK_RANK), (0, 0))).astype(BF16)
    wbr = [w.astype(BF16) for w in (w_br_a, w_br_b, w_br_c)]
    wout = w_out.astype(BF16)
    wg, wu, wd = w_ffn_gate.astype(BF16), w_ffn_up.astype(BF16), w_ffn_down.astype(BF16)
    cos_t, sin_t = _rotary_tables(seq_len)
    dmask, qdec, kdec, sdec = _retention_tables(tb)
    pos = jnp.arange(tb)
    tri = ((pos[:, None] >= pos[None, :]) & (pos[:, None] // CHUNK == pos[None, :] // CHUNK)).astype(BF16)
    nfin = norm_final.reshape(1, d)

    xf = x.reshape(tokens, d)
    for layer in range(DEPTH):
        (qa, ka, va, lfa, oga, qb, kb, vb, ogb, qc, kc, vc, lgc, ogc, gt) = _inproj(
            layer, xf, norm_mix[layer].reshape(1, d), w_main, w_code, w_gates, lb_logits,
            cos_t, sin_t, wup, b_gk[layer].reshape(1, -1), seq_len, tm)
        consts = [gn_a[layer].reshape(1, -1), gn_c[layer].reshape(1, -1), tri, dmask, qdec, kdec, sdec,
                  norm_ffn[layer].reshape(1, d), nfin]
        xf = _mixpost(layer, [qa, ka, va, lfa, oga, qb, kb, vb, ogb, qc, kc, vc, lgc, ogc],
                      gt, xf, consts, wbr + [wout, wg, wu, wd], seq_len, tb)
    return xf.reshape(batch, seq_len, d)
```

```python
import functools

import jax
import jax.numpy as jnp
from jax import lax
from jax.experimental import pallas as pl
from jax.experimental.pallas import tpu as pltpu

F32 = jnp.float32
BF16 = jnp.bfloat16

DEPTH = 4
CHUNK = 64
SUB = 8
NORM_EPS = 1e-6
N_HEADS = 4
HEAD_DV = 128
QK_WIDE = 128
QK_NARROW = 64
ROPE_BASE = 10000.0
GK_RANK = 16
GK_NORMALIZER = 16.0
LOG2E = 1.4426950408889634
LANES = 128
V7X_SCOPED_VMEM_BYTES = 60000 * 1024

_A_Q, _A_F, _A_I, _A_G = 0, 512, 1024, 1536
_B_Q, _B_K, _B_V, _B_G = 2048, 2304, 2560, 3072
_C_Q, _C_K, _C_V, _C_G = 3584, 3840, 4096, 4608
_C_R = 5120
_GATES = _C_R + GK_RANK


def _nt(a, b):
    return lax.dot_general(a, b, (((1,), (1,)), ((), ())), preferred_element_type=F32)


def _tn(a, b):
    return lax.dot_general(a, b, (((0,), (0,)), ((), ())), preferred_element_type=F32)


def _mm(a, b):
    return jnp.dot(a, b, preferred_element_type=F32)


def _one_plus_tanh_half(z):
    return jnp.tanh(0.5 * z) + 1.0


def _silu(z):
    hz = 0.5 * z
    return hz * (jnp.tanh(hz) + 1.0)


def _log_sigmoid(z):
    return jnp.minimum(z, 0.0) - jnp.log1p(jnp.exp(-jnp.abs(z)))


def _rmsnorm(x, w):
    ms = jnp.mean(x * x, axis=-1, keepdims=True)
    return x * lax.rsqrt(ms + NORM_EPS) * w


def _swap_lane_pairs(t):
    n = t.shape[1]
    lane = lax.broadcasted_iota(jnp.int32, t.shape, 1)
    return jnp.where(lane % 2 == 0, pltpu.roll(t, n - 1, 1), pltpu.roll(t, 1, 1))


def _inproj_kernel(layer, x_ref, nw_ref, w_ref, wr_ref, wgt_ref, lbl_ref, cos_ref, sin_ref, wup_ref, bgk_ref,
                   qa_ref, ka_ref, va_ref, lfa_ref, oga_ref,
                   qb_ref, kb_ref, vb_ref, ogb_ref,
                   qc_ref, kc_ref, vc_ref, lgc_ref, ogc_ref, gt_ref):
    xn = _rmsnorm(x_ref[...], nw_ref[...]).astype(BF16)

    def proj(lo, hi):
        return _mm(xn, w_ref[:, lo:hi])

    lbl = lbl_ref[...]
    e = jnp.exp(lbl - jnp.max(lbl, axis=0, keepdims=True))
    p = e / jnp.sum(e, axis=0, keepdims=True)
    c = p[0:1]
    for r in range(1, layer + 1):
        c = c + p[r:r + 1]
    lb = c - p[0:1]

    aq = proj(_A_Q, _A_F)
    qa_ref[...] = (_silu(aq) * (QK_WIDE ** -0.5)).astype(BF16)
    z = proj(_A_F, _A_I)
    t = jnp.exp(-jnp.abs(z))
    pos = z >= 0.0
    log_num = jnp.log(jnp.where(pos, 1.0, lb) + jnp.where(pos, lb, 1.0) * t)
    log_num = jnp.where(jnp.logical_or(pos, lb > 0.0), log_num, z)
    lfa_ref[...] = log_num - jnp.log1p(t)
    ka_ref[...] = ((1.0 - lb) * jnp.where(pos, t, 1.0) / (1.0 + t)).astype(BF16)
    va_ref[...] = proj(_A_I, _A_G).astype(BF16)
    ga = proj(_A_G, _B_Q)
    oga_ref[...] = _silu(ga).astype(BF16)

    cos = cos_ref[...]
    sin = sin_ref[...]
    qb = proj(_B_Q, _B_K)
    qb_ref[...] = (qb * cos + _swap_lane_pairs(qb) * sin).astype(BF16)
    kb = proj(_B_K, _B_V)
    kb_ref[...] = ((kb * cos + _swap_lane_pairs(kb) * sin) * (QK_NARROW ** -0.5)).astype(BF16)
    vb_ref[...] = proj(_B_V, _B_G).astype(BF16)
    gb = proj(_B_G, _C_Q)
    ogb_ref[...] = _silu(gb).astype(BF16)

    qc_ref[...] = (proj(_C_Q, _C_K) * (QK_NARROW ** -0.5)).astype(BF16)
    kc_ref[...] = proj(_C_K, _C_V).astype(BF16)
    vc_ref[...] = proj(_C_V, _C_G).astype(BF16)
    gc = proj(_C_G, _C_R)
    ogc_ref[...] = _silu(gc).astype(BF16)
    code = _mm(xn, wr_ref[...]).astype(BF16)
    gk = _mm(code, wup_ref[...]) + bgk_ref[...]
    lgc_ref[...] = _log_sigmoid(gk) * (1.0 / GK_NORMALIZER)

    gt_ref[...] = _mm(xn, wgt_ref[...]).astype(BF16)


def _layer_block(a, layer):
    zeros = (0,) * (a.ndim - 1)
    return pl.BlockSpec((None,) + a.shape[1:], lambda *_: (layer,) + zeros, pipeline_mode=pl.Buffered(1))


def _inproj(layer, xf, nw, w, wr, wgt, lbl, cos_t, sin_t, wup, bgk, seq_len, tm):
    tokens, d = xf.shape
    nt = tokens // tm
    per_seq = seq_len // tm

    def rows(width, dtype):
        return jax.ShapeDtypeStruct((tokens, width), dtype), pl.BlockSpec((tm, width), lambda i: (i, 0))

    wide, narrow = N_HEADS * HEAD_DV, N_HEADS * QK_NARROW
    outs = [rows(wide, BF16), rows(wide, BF16), rows(wide, BF16), rows(wide, F32), rows(wide, BF16),
            rows(narrow, BF16), rows(narrow, BF16), rows(wide, BF16), rows(wide, BF16),
            rows(narrow, BF16), rows(narrow, BF16), rows(wide, BF16), rows(narrow, F32), rows(wide, BF16),
            rows(wgt.shape[-1], BF16)]
    const = lambda shape: pl.BlockSpec(shape, lambda i: (0,) * len(shape), pipeline_mode=pl.Buffered(1))
    in_specs = [
        pl.BlockSpec((tm, d), lambda i: (i, 0)),
        const((1, d)),
        _layer_block(w, layer),
        _layer_block(wr, layer),
        _layer_block(wgt, layer),
        const(lbl.shape),
        pl.BlockSpec((tm, narrow), lambda i: (i % per_seq, 0)),
        pl.BlockSpec((tm, narrow), lambda i: (i % per_seq, 0)),
        _layer_block(wup, layer),
        const(bgk.shape),
    ]
    return pl.pallas_call(
        functools.partial(_inproj_kernel, layer),
        grid=(nt,),
        in_specs=in_specs,
        out_specs=[o[1] for o in outs],
        out_shape=[o[0] for o in outs],
        compiler_params=pltpu.CompilerParams(
            dimension_semantics=("parallel",), vmem_limit_bytes=V7X_SCOPED_VMEM_BYTES),
        name="inproj",
    )(xf, nw, w, wr, wgt, lbl, cos_t, sin_t, wup, bgk)


def _head_masks(nh):
    lane = lax.broadcasted_iota(jnp.int32, (1, LANES), 1)
    return [(lane // (LANES // nh)) == m for m in range(nh)]


def _stack_heads(x, masks):
    if len(masks) == 1:
        return x
    return jnp.concatenate([jnp.where(m, x, 0.0) for m in masks], axis=0)


def _cumsum_rows_f32(tri, g):
    hi = g.astype(BF16)
    r1 = g - hi.astype(F32)
    mid = r1.astype(BF16)
    lo = (r1 - mid.astype(F32)).astype(BF16)
    return _mm(tri, hi) + _mm(tri, mid) + _mm(tri, lo)


def _gla_chunk_scores(qs, bs, kf_sc, b_sc, r0, masks):
    ng, nh = len(qs), len(masks)
    lane_c = lax.broadcasted_iota(jnp.int32, (SUB, CHUNK), 1)
    row_c = lax.broadcasted_iota(jnp.int32, (SUB, CHUNK), 0)
    blocks = [[[] for _ in range(nh)] for _ in range(ng)]
    kk_rows = [[] for _ in range(ng)]
    for blk in range(CHUNK // SUB):
        lo = blk * SUB
        causal = (lane_c - lo) <= row_c
        dg = [[jnp.zeros((SUB, CHUNK), F32) for _ in range(nh)] for _ in range(ng)]
        for jj in range(SUB):
            j = lo + jj
            hit = lane_c == j
            for gi in range(ng):
                lanes = slice(gi * LANES, (gi + 1) * LANES)
                kj = kf_sc[r0 + j:r0 + j + 1, lanes]
                bj = b_sc[r0 + j:r0 + j + 1, lanes]
                pr = qs[gi][lo:lo + SUB] * kj * jnp.exp2(bs[gi][lo:lo + SUB] - bj)
                for m in range(nh):
                    pm = pr if nh == 1 else jnp.where(masks[m], pr, 0.0)
                    dg[gi][m] = jnp.where(hit, jnp.sum(pm, axis=-1, keepdims=True), dg[gi][m])
        for gi in range(ng):
            lanes = slice(gi * LANES, (gi + 1) * LANES)
            blk_dg = [jnp.where(causal, d, 0.0) for d in dg[gi]]
            if blk > 0:
                ref_row = b_sc[r0 + lo - 1:r0 + lo, lanes]
                qp = qs[gi][lo:lo + SUB] * jnp.exp2(bs[gi][lo:lo + SUB] - ref_row)
                if blk > 1:
                    step = jnp.exp2(ref_row - b_sc[r0 + lo - SUB - 1:r0 + lo - SUB, lanes])
                    kk_rows[gi] = [kr * step for kr in kk_rows[gi]]
                kk_rows[gi].append(kf_sc[r0 + lo - SUB:r0 + lo, lanes]
                                   * jnp.exp2(ref_row - bs[gi][lo - SUB:lo]))
                kk = jnp.concatenate(kk_rows[gi] + [jnp.zeros((CHUNK - lo, LANES), F32)], axis=0).astype(BF16)
                lhs = _stack_heads(qp, masks)
                if lhs.shape[0] < 16:
                    lhs = jnp.concatenate([lhs, jnp.zeros((16 - lhs.shape[0], LANES), F32)], axis=0)
                off = _nt(lhs.astype(BF16), kk)
                blk_dg = [off[m * SUB:(m + 1) * SUB] + blk_dg[m] for m in range(nh)]
            for m in range(nh):
                blocks[gi][m].append(blk_dg[m])
    return [[jnp.concatenate(bl, axis=0) for bl in grp] for grp in blocks]


def _gla_stage(nh, q_ref, k_ref, v_ref, g_ref, og_ref, gn_ref, tri_ref, o_ref, o_base, st_ref, b_sc, kf_sc):
    tb = q_ref.shape[0]
    n_groups = q_ref.shape[1] // LANES
    masks = _head_masks(nh)
    gn = gn_ref[...]
    b_sc[...] = _cumsum_rows_f32(tri_ref[...], g_ref[...]) * LOG2E
    kf_sc[...] = k_ref[...].astype(F32)

    for c in range(tb // CHUNK):
        r0 = c * CHUNK
        rows = slice(r0, r0 + CHUNK)
        qs = [q_ref[rows, gi * LANES:(gi + 1) * LANES].astype(F32) for gi in range(n_groups)]
        bs = [b_sc[rows, gi * LANES:(gi + 1) * LANES] for gi in range(n_groups)]
        scores = _gla_chunk_scores(qs, bs, kf_sc, b_sc, r0, masks)
        for gi in range(n_groups):
            lanes = slice(gi * LANES, (gi + 1) * LANES)
            b = bs[gi]
            b_last = b_sc[r0 + CHUNK - 1:r0 + CHUNK, lanes]
            st = st_ref[gi]
            inter = _nt(_stack_heads(qs[gi] * jnp.exp2(b), masks).astype(BF16), st.astype(BF16))
            vs = []
            for m in range(nh):
                vl = slice((gi * nh + m) * HEAD_DV, (gi * nh + m + 1) * HEAD_DV)
                v = v_ref[rows, vl]
                vs.append(v)
                o = _mm(scores[gi][m].astype(BF16), v) + inter[m * CHUNK:(m + 1) * CHUNK]
                y = _rmsnorm(o, gn) * og_ref[rows, vl].astype(F32)
                o_ref[rows, o_base + vl.start:o_base + vl.stop] = y.astype(BF16)
            kd = _stack_heads(kf_sc[rows, lanes] * jnp.exp2(b_last - b), masks).astype(BF16)
            vcat = vs[0] if nh == 1 else jnp.concatenate(vs, axis=0)
            st_ref[gi] = st * jnp.exp2(b_last) + _tn(vcat, kd)


def _ret_stage(q_ref, k_ref, v_ref, og_ref, dmask_ref, qdec_ref, kdec_ref, sdec_ref, o_ref, o_base, st_ref):
    nh = 2
    cb = q_ref.shape[0]
    masks = _head_masks(nh)
    for gi in range(q_ref.shape[1] // LANES):
        lanes = slice(gi * LANES, (gi + 1) * LANES)
        q = q_ref[:, lanes].astype(F32)
        k = k_ref[:, lanes].astype(F32)
        sc = _nt(_stack_heads(q, masks).astype(BF16), k_ref[:, lanes]) * dmask_ref[gi]
        st = st_ref[gi]
        inter = _nt(_stack_heads(q * qdec_ref[gi], masks).astype(BF16), st.astype(BF16))
        vs = []
        for m in range(nh):
            vl = slice((gi * nh + m) * HEAD_DV, (gi * nh + m + 1) * HEAD_DV)
            v = v_ref[:, vl]
            vs.append(v)
            o = _mm(sc[m * cb:(m + 1) * cb].astype(BF16), v) + inter[m * cb:(m + 1) * cb]
            mu = jnp.mean(o, axis=-1, keepdims=True)
            var = jnp.mean(jnp.square(o - mu), axis=-1, keepdims=True)
            y = (o - mu) * lax.rsqrt(var + NORM_EPS) * og_ref[:, vl].astype(F32)
            o_ref[:, o_base + vl.start:o_base + vl.stop] = y.astype(BF16)
        kd = _stack_heads(k * kdec_ref[gi], masks).astype(BF16)
        st_ref[gi] = st * sdec_ref[gi] + _tn(jnp.concatenate(vs, axis=0), kd)


def _post_stage(final, o_branches, gt_ref, x_ref, wbr_refs, wout_ref, nffn_ref, wg_ref, wu_ref, wd_ref,
                nfin_ref, out_ref):
    d = x_ref.shape[1]
    merged = None
    for i, o_branch in enumerate(o_branches):
        gate2 = _one_plus_tanh_half(gt_ref[:, i * d:(i + 1) * d].astype(F32))
        term = gate2 * _mm(o_branch, wbr_refs[i][...])
        merged = term if merged is None else merged + term
    h = x_ref[...] + _mm((0.5 * merged).astype(BF16), wout_ref[...])
    hn = _rmsnorm(h, nffn_ref[...]).astype(BF16)
    g = _mm(hn, wg_ref[...])
    u = _mm(hn, wu_ref[...])
    y = h + _mm((_silu(g) * u).astype(BF16), wd_ref[...])
    if final:
        y = _rmsnorm(y, nfin_ref[...])
    out_ref[...] = y


def _mixpost_kernel(final, per_seq,
                    qa_ref, ka_ref, va_ref, lfa_ref, oga_ref, qb_ref, kb_ref, vb_ref, ogb_ref,
                    qc_ref, kc_ref, vc_ref, lgc_ref, ogc_ref, gt_ref, x_ref,
                    gna_ref, gnc_ref, tri_ref, dmask_ref, qdec_ref, kdec_ref, sdec_ref, nffn_ref, nfin_ref,
                    wbra_ref, wbrb_ref, wbrc_ref, wout_ref, wg_ref, wu_ref, wd_ref,
                    out_ref,
                    sta_ref, stb_ref, stc_ref, ba_sc, kfa_sc, bc_sc, kfc_sc, o_sc):
    s = pl.program_id(0)

    @pl.when(s % per_seq == 0)
    def _():
        sta_ref[...] = jnp.zeros_like(sta_ref)
        stb_ref[...] = jnp.zeros_like(stb_ref)
        stc_ref[...] = jnp.zeros_like(stc_ref)

    @pl.when(s == 0)
    def _():
        o_sc[...] = jnp.zeros_like(o_sc)

    width = N_HEADS * HEAD_DV
    o_prev = [o_sc[:, i * width:(i + 1) * width] for i in range(3)]
    _gla_stage(1, qa_ref, ka_ref, va_ref, lfa_ref, oga_ref, gna_ref, tri_ref, o_sc, 0, sta_ref, ba_sc, kfa_sc)
    _ret_stage(qb_ref, kb_ref, vb_ref, ogb_ref, dmask_ref, qdec_ref, kdec_ref, sdec_ref, o_sc, width,
               stb_ref)
    _gla_stage(2, qc_ref, kc_ref, vc_ref, lgc_ref, ogc_ref, gnc_ref, tri_ref, o_sc, 2 * width, stc_ref,
               bc_sc, kfc_sc)
    _post_stage(final, o_prev, gt_ref, x_ref, (wbra_ref, wbrb_ref, wbrc_ref), wout_ref, nffn_ref, wg_ref,
                wu_ref, wd_ref, nfin_ref, out_ref)


def _mixpost(layer, mix_rows, gt, xf, consts, weights, seq_len, tb):
    tokens, d = xf.shape
    n_blocks = tokens // tb
    per_seq = seq_len // tb
    width = N_HEADS * HEAD_DV
    cur = lambda a: pl.BlockSpec((tb, a.shape[1]), lambda s: (jnp.minimum(s, n_blocks - 1), 0))
    prev = lambda a: pl.BlockSpec((tb, a.shape[1]), lambda s: (jnp.maximum(s - 1, 0), 0))
    const = lambda a: pl.BlockSpec(a.shape, lambda s: (0,) * a.ndim, pipeline_mode=pl.Buffered(1))
    state = lambda n: pltpu.VMEM((n, HEAD_DV, LANES), F32)
    rows_f32 = lambda w: pltpu.VMEM((tb, w), F32)
    return pl.pallas_call(
        functools.partial(_mixpost_kernel, layer == DEPTH - 1, per_seq),
        grid=(n_blocks + 1,),
        in_specs=([cur(a) for a in mix_rows] + [prev(gt), prev(xf)] + [const(a) for a in consts]
                  + [_layer_block(a, layer) for a in weights]),
        out_specs=pl.BlockSpec((tb, d), lambda s: (jnp.maximum(s - 1, 0), 0)),
        out_shape=jax.ShapeDtypeStruct((tokens, d), F32),
        scratch_shapes=[state(4), state(2), state(2), rows_f32(512), rows_f32(512), rows_f32(256), rows_f32(256),
                        pltpu.VMEM((tb, 3 * width), BF16)],
        compiler_params=pltpu.CompilerParams(
            dimension_semantics=("arbitrary",), vmem_limit_bytes=V7X_SCOPED_VMEM_BYTES),
        name="mixpost",
    )(*mix_rows, gt, xf, *consts, *weights)


def _rotary_tables(seq_len):
    inv_freq = 1.0 / (ROPE_BASE ** jnp.linspace(0.0, 1.0, QK_NARROW // 2, dtype=F32))
    ang = jnp.arange(seq_len, dtype=F32)[:, None] * inv_freq[None, :]
    sin, cos = jnp.sin(ang), jnp.cos(ang)
    cos_t = jnp.tile(jnp.repeat(cos, 2, axis=1), (1, N_HEADS))
    sin_t = jnp.tile(jnp.stack([-sin, sin], axis=-1).reshape(seq_len, QK_NARROW), (1, N_HEADS))
    return cos_t, sin_t


def _retention_tables(cb):
    log_gamma = jnp.log(1.0 - 2.0 ** (-5.0 - jnp.arange(N_HEADS, dtype=F32)))
    pos = jnp.arange(cb, dtype=F32)
    diff = pos[:, None] - pos[None, :]
    dmask = jnp.where(diff >= 0, jnp.exp(diff[None] * log_gamma[:, None, None]), 0.0)
    lane_gamma = jnp.repeat(log_gamma, QK_NARROW).reshape(N_HEADS // 2, 1, LANES)
    qdec = jnp.exp((pos[None, :, None] + 1.0) * lane_gamma)
    kdec = jnp.exp((cb - 1.0 - pos[None, :, None]) * lane_gamma)
    sdec = jnp.exp(float(cb) * lane_gamma)
    return dmask.reshape(N_HEADS // 2, 2 * cb, cb), qdec, kdec, sdec


def kernel(x, norm_mix, w_in, lb_logits, w_gk_up, b_gk, gn_a, gn_c, w_br_a, w_br_b, w_br_c, w_out,
           norm_ffn, w_ffn_gate, w_ffn_up, w_ffn_down, norm_final):
    batch, seq_len, d = x.shape
    tokens = batch * seq_len
    tm = min(512, seq_len)
    tb = min(256, seq_len)

    w_main = w_in[:, :, :_C_R].astype(BF16)
    w_code = jnp.pad(w_in[:, :, _C_R:_GATES], ((0, 0), (0, 0), (0, LANES - GK_RANK))).astype(BF16)
    w_gates = w_in[:, :, _GATES:].astype(BF16)
    wup = jnp.pad(w_gk_up, ((0, 0), (0, LANES - GK_RANK), (0, 0))).astype(BF16)
    wbr = [w.astype(BF16) for w in (w_br_a, w_br_b, w_br_c)]
    wout = w_out.astype(BF16)
    wg, wu, wd = w_ffn_gate.astype(BF16), w_ffn_up.astype(BF16), w_ffn_down.astype(BF16)
    cos_t, sin_t = _rotary_tables(seq_len)
    dmask, qdec, kdec, sdec = _retention_tables(tb)
    pos = jnp.arange(tb)
    tri = ((pos[:, None] >= pos[None, :]) & (pos[:, None] // CHUNK == pos[None, :] // CHUNK)).astype(BF16)
    nfin = norm_final.reshape(1, d)

    xf = x.reshape(tokens, d)
    for layer in range(DEPTH):
        (qa, ka, va, lfa, oga, qb, kb, vb, ogb, qc, kc, vc, lgc, ogc, gt) = _inproj(
            layer, xf, norm_mix[layer].reshape(1, d), w_main, w_code, w_gates, lb_logits,
            cos_t, sin_t, wup, b_gk[layer].reshape(1, -1), seq_len, tm)
        consts = [gn_a[layer].reshape(1, -1), gn_c[layer].reshape(1, -1), tri, dmask, qdec, kdec, sdec,
                  norm_ffn[layer].reshape(1, d), nfin]
        xf = _mixpost(layer, [qa, ka, va, lfa, oga, qb, kb, vb, ogb, qc, kc, vc, lgc, ogc],
                      gt, xf, consts, wbr + [wout, wg, wu, wd], seq_len, tb)
    return xf.reshape(batch, seq_len, d)
```

```python
import functools

import jax
import jax.numpy as jnp
from jax import lax
from jax.experimental import pallas as pl
from jax.experimental.pallas import tpu as pltpu

F32 = jnp.float32
BF16 = jnp.bfloat16

DEPTH = 4
CHUNK = 64
SUB = 8
NORM_EPS = 1e-6
N_HEADS = 4
HEAD_DV = 128
QK_WIDE = 128
QK_NARROW = 64
ROPE_BASE = 10000.0
GK_RANK = 16
GK_NORMALIZER = 16.0
LOG2E = 1.4426950408889634
LANES = 128
V7X_SCOPED_VMEM_BYTES = 60000 * 1024

_A_Q, _A_F, _A_I, _A_G = 0, 512, 1024, 1536
_B_Q, _B_K, _B_V, _B_G = 2048, 2304, 2560, 3072
_C_Q, _C_K, _C_V, _C_G = 3584, 3840, 4096, 4608
_C_R = 5120
_GATES = _C_R + GK_RANK


def _nt(a, b):
    return lax.dot_general(a, b, (((1,), (1,)), ((), ())), preferred_element_type=F32)


def _tn(a, b):
    return lax.dot_general(a, b, (((0,), (0,)), ((), ())), preferred_element_type=F32)


def _mm(a, b):
    return jnp.dot(a, b, preferred_element_type=F32)


def _sigmoid(z):
    return 0.5 * jnp.tanh(0.5 * z) + 0.5


def _log_sigmoid(z):
    return jnp.minimum(z, 0.0) - jnp.log1p(jnp.exp(-jnp.abs(z)))


def _rmsnorm(x, w):
    ms = jnp.mean(x * x, axis=-1, keepdims=True)
    return x * lax.rsqrt(ms + NORM_EPS) * w


def _swap_lane_pairs(t):
    n = t.shape[1]
    lane = lax.broadcasted_iota(jnp.int32, t.shape, 1)
    return jnp.where(lane % 2 == 0, pltpu.roll(t, n - 1, 1), pltpu.roll(t, 1, 1))


def _inproj_kernel(layer, x_ref, nw_ref, w_ref, wr_ref, wgt_ref, lbl_ref, cos_ref, sin_ref, wup_ref, bgk_ref,
                   qa_ref, ka_ref, va_ref, lfa_ref, oga_ref,
                   qb_ref, kb_ref, vb_ref, ogb_ref,
                   qc_ref, kc_ref, vc_ref, lgc_ref, ogc_ref, gt_ref):
    xn = _rmsnorm(x_ref[...], nw_ref[...]).astype(BF16)

    def proj(lo, hi):
        return _mm(xn, w_ref[:, lo:hi])

    lbl = lbl_ref[...]
    e = jnp.exp(lbl - jnp.max(lbl, axis=0, keepdims=True))
    p = e / jnp.sum(e, axis=0, keepdims=True)
    c = p[0:1]
    for r in range(1, layer + 1):
        c = c + p[r:r + 1]
    lb = c - p[0:1]

    aq = proj(_A_Q, _A_F)
    qa_ref[...] = (aq * _sigmoid(aq) * (QK_WIDE ** -0.5)).astype(BF16)
    z = proj(_A_F, _A_I)
    t = jnp.exp(-jnp.abs(z))
    pos = z >= 0.0
    log_num = jnp.log(jnp.where(pos, 1.0, lb) + jnp.where(pos, lb, 1.0) * t)
    log_num = jnp.where(jnp.logical_or(pos, lb > 0.0), log_num, z)
    lfa_ref[...] = log_num - jnp.log1p(t)
    ka_ref[...] = ((1.0 - lb) * jnp.where(pos, t, 1.0) / (1.0 + t)).astype(BF16)
    va_ref[...] = proj(_A_I, _A_G).astype(BF16)
    ga = proj(_A_G, _B_Q)
    oga_ref[...] = (ga * _sigmoid(ga)).astype(BF16)

    cos = cos_ref[...]
    sin = sin_ref[...]
    qb = proj(_B_Q, _B_K)
    qb_ref[...] = (qb * cos + _swap_lane_pairs(qb) * sin).astype(BF16)
    kb = proj(_B_K, _B_V)
    kb_ref[...] = ((kb * cos + _swap_lane_pairs(kb) * sin) * (QK_NARROW ** -0.5)).astype(BF16)
    vb_ref[...] = proj(_B_V, _B_G).astype(BF16)
    gb = proj(_B_G, _C_Q)
    ogb_ref[...] = (gb * _sigmoid(gb)).astype(BF16)

    qc_ref[...] = (proj(_C_Q, _C_K) * (QK_NARROW ** -0.5)).astype(BF16)
    kc_ref[...] = proj(_C_K, _C_V).astype(BF16)
    vc_ref[...] = proj(_C_V, _C_G).astype(BF16)
    gc = proj(_C_G, _C_R)
    ogc_ref[...] = (gc * _sigmoid(gc)).astype(BF16)
    code = _mm(xn, wr_ref[...]).astype(BF16)
    gk = _mm(code, wup_ref[...]) + bgk_ref[...]
    lgc_ref[...] = _log_sigmoid(gk) * (1.0 / GK_NORMALIZER)

    gt_ref[...] = _mm(xn, wgt_ref[...]).astype(BF16)


def _layer_block(a, layer):
    zeros = (0,) * (a.ndim - 1)
    return pl.BlockSpec((None,) + a.shape[1:], lambda *_: (layer,) + zeros, pipeline_mode=pl.Buffered(1))


def _inproj(layer, xf, nw, w, wr, wgt, lbl, cos_t, sin_t, wup, bgk, seq_len, tm):
    tokens, d = xf.shape
    nt = tokens // tm
    per_seq = seq_len // tm

    def rows(width, dtype):
        return jax.ShapeDtypeStruct((tokens, width), dtype), pl.BlockSpec((tm, width), lambda i: (i, 0))

    wide, narrow = N_HEADS * HEAD_DV, N_HEADS * QK_NARROW
    outs = [rows(wide, BF16), rows(wide, BF16), rows(wide, BF16), rows(wide, F32), rows(wide, BF16),
            rows(narrow, BF16), rows(narrow, BF16), rows(wide, BF16), rows(wide, BF16),
            rows(narrow, BF16), rows(narrow, BF16), rows(wide, BF16), rows(narrow, F32), rows(wide, BF16),
            rows(wgt.shape[-1], BF16)]
    const = lambda shape: pl.BlockSpec(shape, lambda i: (0,) * len(shape), pipeline_mode=pl.Buffered(1))
    in_specs = [
        pl.BlockSpec((tm, d), lambda i: (i, 0)),
        const((1, d)),
        _layer_block(w, layer),
        _layer_block(wr, layer),
        _layer_block(wgt, layer),
        const(lbl.shape),
        pl.BlockSpec((tm, narrow), lambda i: (i % per_seq, 0)),
        pl.BlockSpec((tm, narrow), lambda i: (i % per_seq, 0)),
        _layer_block(wup, layer),
        const(bgk.shape),
    ]
    return pl.pallas_call(
        functools.partial(_inproj_kernel, layer),
        grid=(nt,),
        in_specs=in_specs,
        out_specs=[o[1] for o in outs],
        out_shape=[o[0] for o in outs],
        compiler_params=pltpu.CompilerParams(
            dimension_semantics=("parallel",), vmem_limit_bytes=V7X_SCOPED_VMEM_BYTES),
        name="inproj",
    )(xf, nw, w, wr, wgt, lbl, cos_t, sin_t, wup, bgk)


def _head_masks(nh):
    lane = lax.broadcasted_iota(jnp.int32, (1, LANES), 1)
    return [(lane // (LANES // nh)) == m for m in range(nh)]


def _stack_heads(x, masks):
    if len(masks) == 1:
        return x
    return jnp.concatenate([jnp.where(m, x, 0.0) for m in masks], axis=0)


def _cumsum_rows(tri, g):
    hi = g.astype(BF16)
    r1 = g - hi.astype(F32)
    mid = r1.astype(BF16)
    return _mm(tri, hi) + _mm(tri, mid)


def _gla_chunk_scores(qs, bs, kf_sc, b_sc, r0, masks):
    ng, nh = len(qs), len(masks)
    lane_c = lax.broadcasted_iota(jnp.int32, (SUB, CHUNK), 1)
    row_c = lax.broadcasted_iota(jnp.int32, (SUB, CHUNK), 0)
    blocks = [[[] for _ in range(nh)] for _ in range(ng)]
    kk_rows = [[] for _ in range(ng)]
    for blk in range(CHUNK // SUB):
        lo = blk * SUB
        causal = (lane_c - lo) <= row_c
        dg = [[jnp.zeros((SUB, CHUNK), F32) for _ in range(nh)] for _ in range(ng)]
        for jj in range(SUB):
            j = lo + jj
            hit = lane_c == j
            for gi in range(ng):
                lanes = slice(gi * LANES, (gi + 1) * LANES)
                kj = kf_sc[r0 + j:r0 + j + 1, lanes]
                bj = b_sc[r0 + j:r0 + j + 1, lanes]
                pr = qs[gi][lo:lo + SUB] * kj * jnp.exp2(bs[gi][lo:lo + SUB] - bj)
                for m in range(nh):
                    pm = pr if nh == 1 else jnp.where(masks[m], pr, 0.0)
                    dg[gi][m] = jnp.where(hit, jnp.sum(pm, axis=-1, keepdims=True), dg[gi][m])
        for gi in range(ng):
            lanes = slice(gi * LANES, (gi + 1) * LANES)
            blk_dg = [jnp.where(causal, d, 0.0) for d in dg[gi]]
            if blk > 0:
                ref_row = b_sc[r0 + lo - 1:r0 + lo, lanes]
                qp = qs[gi][lo:lo + SUB] * jnp.exp2(bs[gi][lo:lo + SUB] - ref_row)
                if blk > 1:
                    step = jnp.exp2(ref_row - b_sc[r0 + lo - SUB - 1:r0 + lo - SUB, lanes])
                    kk_rows[gi] = [kr * step for kr in kk_rows[gi]]
                kk_rows[gi].append(kf_sc[r0 + lo - SUB:r0 + lo, lanes]
                                   * jnp.exp2(ref_row - bs[gi][lo - SUB:lo]))
                kk = jnp.concatenate(kk_rows[gi] + [jnp.zeros((CHUNK - lo, LANES), F32)], axis=0).astype(BF16)
                lhs = _stack_heads(qp, masks)
                if lhs.shape[0] < 16:
                    lhs = jnp.concatenate([lhs, jnp.zeros((16 - lhs.shape[0], LANES), F32)], axis=0)
                off = _nt(lhs.astype(BF16), kk)
                blk_dg = [off[m * SUB:(m + 1) * SUB] + blk_dg[m] for m in range(nh)]
            for m in range(nh):
                blocks[gi][m].append(blk_dg[m])
    return [[jnp.concatenate(bl, axis=0) for bl in grp] for grp in blocks]


def _gla_stage(nh, q_ref, k_ref, v_ref, g_ref, og_ref, gn_ref, tri_ref, o_ref, o_base, st_ref, b_sc, kf_sc):
    tb = q_ref.shape[0]
    n_groups = q_ref.shape[1] // LANES
    masks = _head_masks(nh)
    gn = gn_ref[...]
    b_sc[...] = _cumsum_rows(tri_ref[...], g_ref[...]) * LOG2E
    kf_sc[...] = k_ref[...].astype(F32)

    for c in range(tb // CHUNK):
        r0 = c * CHUNK
        rows = slice(r0, r0 + CHUNK)
        qs = [q_ref[rows, gi * LANES:(gi + 1) * LANES].astype(F32) for gi in range(n_groups)]
        bs = [b_sc[rows, gi * LANES:(gi + 1) * LANES] for gi in range(n_groups)]
        scores = _gla_chunk_scores(qs, bs, kf_sc, b_sc, r0, masks)
        for gi in range(n_groups):
            lanes = slice(gi * LANES, (gi + 1) * LANES)
            b = bs[gi]
            b_last = b_sc[r0 + CHUNK - 1:r0 + CHUNK, lanes]
            st = st_ref[gi]
            inter = _nt(_stack_heads(qs[gi] * jnp.exp2(b), masks).astype(BF16), st.astype(BF16))
            vs = []
            for m in range(nh):
                vl = slice((gi * nh + m) * HEAD_DV, (gi * nh + m + 1) * HEAD_DV)
                v = v_ref[rows, vl]
                vs.append(v)
                o = _mm(scores[gi][m].astype(BF16), v) + inter[m * CHUNK:(m + 1) * CHUNK]
                y = _rmsnorm(o, gn) * og_ref[rows, vl].astype(F32)
                o_ref[rows, o_base + vl.start:o_base + vl.stop] = y.astype(BF16)
            kd = _stack_heads(kf_sc[rows, lanes] * jnp.exp2(b_last - b), masks).astype(BF16)
            vcat = vs[0] if nh == 1 else jnp.concatenate(vs, axis=0)
            st_ref[gi] = st * jnp.exp2(b_last) + _tn(vcat, kd)


def _ret_stage(q_ref, k_ref, v_ref, og_ref, dmask_ref, qdec_ref, kdec_ref, sdec_ref, o_ref, o_base, st_ref):
    nh = 2
    cb = q_ref.shape[0]
    masks = _head_masks(nh)
    for gi in range(q_ref.shape[1] // LANES):
        lanes = slice(gi * LANES, (gi + 1) * LANES)
        q = q_ref[:, lanes].astype(F32)
        k = k_ref[:, lanes].astype(F32)
        sc = _nt(_stack_heads(q, masks).astype(BF16), k_ref[:, lanes]) * dmask_ref[gi]
        st = st_ref[gi]
        inter = _nt(_stack_heads(q * qdec_ref[gi], masks).astype(BF16), st.astype(BF16))
        vs = []
        for m in range(nh):
            vl = slice((gi * nh + m) * HEAD_DV, (gi * nh + m + 1) * HEAD_DV)
            v = v_ref[:, vl]
            vs.append(v)
            o = _mm(sc[m * cb:(m + 1) * cb].astype(BF16), v) + inter[m * cb:(m + 1) * cb]
            mu = jnp.mean(o, axis=-1, keepdims=True)
            var = jnp.mean(jnp.square(o - mu), axis=-1, keepdims=True)
            y = (o - mu) * lax.rsqrt(var + NORM_EPS) * og_ref[:, vl].astype(F32)
            o_ref[:, o_base + vl.start:o_base + vl.stop] = y.astype(BF16)
        kd = _stack_heads(k * kdec_ref[gi], masks).astype(BF16)
        st_ref[gi] = st * sdec_ref[gi] + _tn(jnp.concatenate(vs, axis=0), kd)


def _post_stage(final, o_branches, gt_ref, x_ref, wbr_refs, wout_ref, nffn_ref, wg_ref, wu_ref, wd_ref,
                nfin_ref, out_ref):
    d = x_ref.shape[1]
    merged = None
    for i, o_branch in enumerate(o_branches):
        gate = _sigmoid(gt_ref[:, i * d:(i + 1) * d].astype(F32))
        term = gate * _mm(o_branch, wbr_refs[i][...])
        merged = term if merged is None else merged + term
    h = x_ref[...] + _mm(merged.astype(BF16), wout_ref[...])
    hn = _rmsnorm(h, nffn_ref[...]).astype(BF16)
    g = _mm(hn, wg_ref[...])
    u = _mm(hn, wu_ref[...])
    y = h + _mm((g * _sigmoid(g) * u).astype(BF16), wd_ref[...])
    if final:
        y = _rmsnorm(y, nfin_ref[...])
    out_ref[...] = y


def _mixpost_kernel(final, per_seq,
                    qa_ref, ka_ref, va_ref, lfa_ref, oga_ref, qb_ref, kb_ref, vb_ref, ogb_ref,
                    qc_ref, kc_ref, vc_ref, lgc_ref, ogc_ref, gt_ref, x_ref,
                    gna_ref, gnc_ref, tri_ref, dmask_ref, qdec_ref, kdec_ref, sdec_ref, nffn_ref, nfin_ref,
                    wbra_ref, wbrb_ref, wbrc_ref, wout_ref, wg_ref, wu_ref, wd_ref,
                    out_ref,
                    sta_ref, stb_ref, stc_ref, ba_sc, kfa_sc, bc_sc, kfc_sc, o_sc):
    s = pl.program_id(0)

    @pl.when(s % per_seq == 0)
    def _():
        sta_ref[...] = jnp.zeros_like(sta_ref)
        stb_ref[...] = jnp.zeros_like(stb_ref)
        stc_ref[...] = jnp.zeros_like(stc_ref)

    @pl.when(s == 0)
    def _():
        o_sc[...] = jnp.zeros_like(o_sc)

    width = N_HEADS * HEAD_DV
    o_prev = [o_sc[:, i * width:(i + 1) * width] for i in range(3)]
    _gla_stage(1, qa_ref, ka_ref, va_ref, lfa_ref, oga_ref, gna_ref, tri_ref, o_sc, 0, sta_ref, ba_sc, kfa_sc)
    _ret_stage(qb_ref, kb_ref, vb_ref, ogb_ref, dmask_ref, qdec_ref, kdec_ref, sdec_ref, o_sc, width,
               stb_ref)
    _gla_stage(2, qc_ref, kc_ref, vc_ref, lgc_ref, ogc_ref, gnc_ref, tri_ref, o_sc, 2 * width, stc_ref,
               bc_sc, kfc_sc)
    _post_stage(final, o_prev, gt_ref, x_ref, (wbra_ref, wbrb_ref, wbrc_ref), wout_ref, nffn_ref, wg_ref,
                wu_ref, wd_ref, nfin_ref, out_ref)


def _mixpost(layer, mix_rows, gt, xf, consts, weights, seq_len, tb):
    tokens, d = xf.shape
    n_blocks = tokens // tb
    per_seq = seq_len // tb
    width = N_HEADS * HEAD_DV
    cur = lambda a: pl.BlockSpec((tb, a.shape[1]), lambda s: (jnp.minimum(s, n_blocks - 1), 0))
    prev = lambda a: pl.BlockSpec((tb, a.shape[1]), lambda s: (jnp.maximum(s - 1, 0), 0))
    const = lambda a: pl.BlockSpec(a.shape, lambda s: (0,) * a.ndim, pipeline_mode=pl.Buffered(1))
    state = lambda n: pltpu.VMEM((n, HEAD_DV, LANES), F32)
    rows_f32 = lambda w: pltpu.VMEM((tb, w), F32)
    return pl.pallas_call(
        functools.partial(_mixpost_kernel, layer == DEPTH - 1, per_seq),
        grid=(n_blocks + 1,),
        in_specs=([cur(a) for a in mix_rows] + [prev(gt), prev(xf)] + [const(a) for a in consts]
                  + [_layer_block(a, layer) for a in weights]),
        out_specs=pl.BlockSpec((tb, d), lambda s: (jnp.maximum(s - 1, 0), 0)),
        out_shape=jax.ShapeDtypeStruct((tokens, d), F32),
        scratch_shapes=[state(4), state(2), state(2), rows_f32(512), rows_f32(512), rows_f32(256), rows_f32(256),
                        pltpu.VMEM((tb, 3 * width), BF16)],
        compiler_params=pltpu.CompilerParams(
            dimension_semantics=("arbitrary",), vmem_limit_bytes=V7X_SCOPED_VMEM_BYTES),
        name="mixpost",
    )(*mix_rows, gt, xf, *consts, *weights)


def _rotary_tables(seq_len):
    inv_freq = 1.0 / (ROPE_BASE ** jnp.linspace(0.0, 1.0, QK_NARROW // 2, dtype=F32))
    ang = jnp.arange(seq_len, dtype=F32)[:, None] * inv_freq[None, :]
    sin, cos = jnp.sin(ang), jnp.cos(ang)
    cos_t = jnp.tile(jnp.repeat(cos, 2, axis=1), (1, N_HEADS))
    sin_t = jnp.tile(jnp.stack([-sin, sin], axis=-1).reshape(seq_len, QK_NARROW), (1, N_HEADS))
    return cos_t, sin_t


def _retention_tables(cb):
    log_gamma = jnp.log(1.0 - 2.0 ** (-5.0 - jnp.arange(N_HEADS, dtype=F32)))
    pos = jnp.arange(cb, dtype=F32)
    diff = pos[:, None] - pos[None, :]
    dmask = jnp.where(diff >= 0, jnp.exp(diff[None] * log_gamma[:, None, None]), 0.0)
    lane_gamma = jnp.repeat(log_gamma, QK_NARROW).reshape(N_HEADS // 2, 1, LANES)
    qdec = jnp.exp((pos[None, :, None] + 1.0) * lane_gamma)
    kdec = jnp.exp((cb - 1.0 - pos[None, :, None]) * lane_gamma)
    sdec = jnp.exp(float(cb) * lane_gamma)
    return dmask.reshape(N_HEADS // 2, 2 * cb, cb), qdec, kdec, sdec


def kernel(x, norm_mix, w_in, lb_logits, w_gk_up, b_gk, gn_a, gn_c, w_br_a, w_br_b, w_br_c, w_out,
           norm_ffn, w_ffn_gate, w_ffn_up, w_ffn_down, norm_final):
    batch, seq_len, d = x.shape
    tokens = batch * seq_len
    tm = min(512, seq_len)
    tb = min(256, seq_len)

    w_main = w_in[:, :, :_C_R].astype(BF16)
    w_code = jnp.pad(w_in[:, :, _C_R:_GATES], ((0, 0), (0, 0), (0, LANES - GK_RANK))).astype(BF16)
    w_gates = w_in[:, :, _GATES:].astype(BF16)
    wup = jnp.pad(w_gk_up, ((0, 0), (0, LANES - GK_RANK), (0, 0))).astype(BF16)
    wbr = [w.astype(BF16) for w in (w_br_a, w_br_b, w_br_c)]
    wout = w_out.astype(BF16)
    wg, wu, wd = w_ffn_gate.astype(BF16), w_ffn_up.astype(BF16), w_ffn_down.astype(BF16)
    cos_t, sin_t = _rotary_tables(seq_len)
    dmask, qdec, kdec, sdec = _retention_tables(tb)
    pos = jnp.arange(tb)
    tri = ((pos[:, None] >= pos[None, :]) & (pos[:, None] // CHUNK == pos[None, :] // CHUNK)).astype(BF16)
    nfin = norm_final.reshape(1, d)

    xf = x.reshape(tokens, d)
    for layer in range(DEPTH):
        (qa, ka, va, lfa, oga, qb, kb, vb, ogb, qc, kc, vc, lgc, ogc, gt) = _inproj(
            layer, xf, norm_mix[layer].reshape(1, d), w_main, w_code, w_gates, lb_logits,
            cos_t, sin_t, wup, b_gk[layer].reshape(1, -1), seq_len, tm)
        consts = [gn_a[layer].reshape(1, -1), gn_c[layer].reshape(1, -1), tri, dmask, qdec, kdec, sdec,
                  norm_ffn[layer].reshape(1, d), nfin]
        xf = _mixpost(layer, [qa, ka, va, lfa, oga, qb, kb, vb, ogb, qc, kc, vc, lgc, ogc],
                      gt, xf, consts, wbr + [wout, wg, wu, wd], seq_len, tb)
    return xf.reshape(batch, seq_len, d)
```
